```python
import jax, jax.numpy as jnp
from jax import lax
import numpy as np

D_MODEL = 1024
BATCH = 4
SEQ = 8192
DEPTH = 2

CHUNK = 64
PLE_DIM = 256
EPS = 1e-6
POOL_WINDOWS = (2, 4, 8, 16)
POOL_GROUPS = len(POOL_WINDOWS)
POOL_WIDTH = D_MODEL // 4
POOL_GROUP_DIM = POOL_WIDTH // POOL_GROUPS
SGU_BLOCK = 128
SGU_WIDTH = D_MODEL // 4
SGU_GROUPS = 4
SGU_GROUP_DIM = SGU_WIDTH // SGU_GROUPS
HEAD_DIM = 64
ATT_WIDTH = D_MODEL // 2
ATT_HEADS = ATT_WIDTH // HEAD_DIM
LEFT_CHUNKS = 8
BAND = (LEFT_CHUNKS + 1) * CHUNK
REL_CLIP = 128
N_BRANCH = 3
SPLITS = (POOL_WIDTH, SGU_WIDTH, SGU_WIDTH, ATT_WIDTH, ATT_WIDTH, ATT_WIDTH)
IN_WIDTH = sum(SPLITS) + N_BRANCH * D_MODEL
N_GROUPS = 4
EXPERTS_PER_GROUP = 8
N_EXPERTS = N_GROUPS * EXPERTS_PER_GROUP
D_EXPERT = 128
TOP_K = 2

kernel_name = "hybrid_pool_sgu_chunkattn_hmoe_ple"


def rms_norm(x, g):
    xf = x.astype(jnp.float32)
    y = xf * lax.rsqrt(jnp.mean(xf * xf, axis=-1, keepdims=True) + EPS)
    return (y * g.astype(jnp.float32)).astype(x.dtype)


def pool_mixer(a, w_pool, scale):
    B, S, _ = a.shape
    a4 = a.reshape(B, S, POOL_GROUPS, POOL_GROUP_DIM)
    af = a4.astype(jnp.float32)
    cs = jnp.cumsum(af, axis=1)
    pos = jnp.arange(1, S + 1, dtype=jnp.float32)
    means = []
    for gi, w in enumerate(POOL_WINDOWS):
        c = cs[:, :, gi]
        lag = jnp.pad(c, ((0, 0), (w, 0), (0, 0)))[:, :S]
        cnt = jnp.minimum(pos, float(w))[None, :, None]
        means.append((c - lag) / cnt)
    pooled = (jnp.stack(means, axis=2) - af).astype(a.dtype)
    y = jnp.einsum('bsgc,gcd->bsgd', pooled, w_pool)
    return y.reshape(B, S, POOL_WIDTH) * scale


def sgu_mixer(u, v, g_norm, w_s, b_s):
    B, S, _ = u.shape
    nb = S // SGU_BLOCK
    v = rms_norm(v, g_norm)
    vb = v.reshape(B, nb, SGU_BLOCK, SGU_GROUPS, SGU_GROUP_DIM)
    i = jnp.arange(SGU_BLOCK)
    mask = (i[None, :] // CHUNK) <= (i[:, None] // CHUNK)
    w = jnp.where(mask[None], w_s, 0)
    mixed = jnp.einsum('gij,bnjgc->bnigc', w, vb) + b_s.T[None, None, :, :, None]
    return u * mixed.reshape(B, S, SGU_WIDTH)


def chunk_attention(q, k, v, gq, gk, rel_bias):
    B, S, H, Dh = q.shape
    nc = S // CHUNK
    q = rms_norm(q, gq)
    k = rms_norm(k, gk)
    pad = LEFT_CHUNKS * CHUNK
    kp = jnp.pad(k, ((0, 0), (pad, 0), (0, 0), (0, 0))).reshape(B, nc + LEFT_CHUNKS, CHUNK, H, Dh)
    vp = jnp.pad(v, ((0, 0), (pad, 0), (0, 0), (0, 0))).reshape(B, nc + LEFT_CHUNKS, CHUNK, H, Dh)
    qc = q.reshape(B, nc, CHUNK, H, Dh)
    band_idx = jnp.arange(nc)[:, None] + jnp.arange(LEFT_CHUNKS + 1)[None, :]
    valid = jnp.repeat(band_idx >= LEFT_CHUNKS, CHUNK, axis=1)
    rel = jnp.clip(jnp.arange(CHUNK)[:, None] + pad - jnp.arange(BAND)[None, :],
                   -REL_CLIP, REL_CLIP) + REL_CLIP
    bias = rel_bias[:, rel].astype(jnp.float32)
    scale = HEAD_DIM ** -0.5

    def one_sequence(args):
        qb, kb, vb = args
        kband = kb[band_idx].reshape(nc, BAND, H, Dh)
        vband = vb[band_idx].reshape(nc, BAND, H, Dh)
        s = jnp.einsum('nqhd,nkhd->nhqk', qb, kband).astype(jnp.float32) * scale + bias[None]
        s = jnp.where(valid[:, None, None, :], s, -1e30)
        pr = jax.nn.softmax(s, axis=-1).astype(vb.dtype)
        return jnp.einsum('nhqk,nkhd->nqhd', pr, vband)

    o = lax.map(one_sequence, (qc, kp, vp))
    return o.reshape(B, S, H * Dh)


def hier_moe(h, w_gr, b_gr, w_er, b_er, w_e_in, w_e_out):
    def per_sequence(hs):
        S = hs.shape[0]
        g_prob = jax.nn.softmax((hs @ w_gr).astype(jnp.float32) + b_gr, axis=-1)
        g_p, g_idx = lax.top_k(g_prob, 1)
        e_logits = ((hs @ w_er).astype(jnp.float32) + b_er).reshape(S, N_GROUPS, EXPERTS_PER_GROUP)
        e_logits = jnp.einsum('sge,sg->se', e_logits,
                              jax.nn.one_hot(g_idx[:, 0], N_GROUPS, dtype=jnp.float32))
        e_prob = jax.nn.softmax(e_logits, axis=-1)
        e_p, e_idx = lax.top_k(e_prob, TOP_K)
        w = g_p * e_p / jnp.sum(e_p, axis=-1, keepdims=True)
        gid = g_idx * EXPERTS_PER_GROUP + e_idx
        gates = jnp.einsum('sk,ske->se', w, jax.nn.one_hot(gid, N_EXPERTS, dtype=jnp.float32))
        hu = jnp.einsum('sd,edf->sef', hs, w_e_in)
        gt, up = jnp.split(hu, 2, axis=-1)
        act = jax.nn.silu(gt) * up * gates[:, :, None].astype(hs.dtype)
        return jnp.einsum('sef,efd->sd', act, w_e_out)
    return lax.map(per_sequence, h)


def setup_inputs(seed: int = 0) -> dict:
    key = jax.random.key(seed)
    ks = iter(jax.random.split(key, 32))
    f32 = jnp.float32

    def nrm(shape, scale):
        return jax.random.normal(next(ks), shape, f32) * scale

    L = DEPTH
    return {
        "x": nrm((BATCH, SEQ, D_MODEL), 1.0),
        "p": nrm((DEPTH, BATCH, SEQ, PLE_DIM), 1.0),
        "mix_norm_g": 1.0 + nrm((L, D_MODEL), 0.05),
        "w_in": nrm((L, D_MODEL, IN_WIDTH), D_MODEL ** -0.5),
        "pool_w": nrm((L, POOL_GROUPS, POOL_GROUP_DIM, POOL_GROUP_DIM), POOL_GROUP_DIM ** -0.5),
        "pool_scale": 1.0 + nrm((L, POOL_WIDTH), 0.1),
        "sgu_norm_g": 1.0 + nrm((L, SGU_WIDTH), 0.05),
        "sgu_w": nrm((L, SGU_GROUPS, SGU_BLOCK, SGU_BLOCK), SGU_BLOCK ** -0.5),
        "sgu_b": 1.0 + nrm((L, SGU_GROUPS, SGU_BLOCK), 0.1),
        "q_norm_g": 1.0 + nrm((L, HEAD_DIM), 0.05),
        "k_norm_g": 1.0 + nrm((L, HEAD_DIM), 0.05),
        "rel_bias": nrm((L, ATT_HEADS, 2 * REL_CLIP + 1), 0.5),
        "gate_b": nrm((L, N_BRANCH, D_MODEL), 0.1),
        "w_branch_a": nrm((L, POOL_WIDTH, D_MODEL), POOL_WIDTH ** -0.5),
        "w_branch_b": nrm((L, SGU_WIDTH, D_MODEL), SGU_WIDTH ** -0.5),
        "w_branch_c": nrm((L, ATT_WIDTH, D_MODEL), ATT_WIDTH ** -0.5),
        "w_out": nrm((L, D_MODEL, D_MODEL), D_MODEL ** -0.5),
        "ffn_norm_g": 1.0 + nrm((L, D_MODEL), 0.05),
        "w_group_router": nrm((L, D_MODEL, N_GROUPS), D_MODEL ** -0.5),
        "b_group_router": nrm((L, N_GROUPS), 0.01),
        "w_expert_router": nrm((L, D_MODEL, N_EXPERTS), D_MODEL ** -0.5),
        "b_expert_router": nrm((L, N_EXPERTS), 0.01),
        "w_expert_in": nrm((L, N_EXPERTS, D_MODEL, 2 * D_EXPERT), D_MODEL ** -0.5),
        "w_expert_out": nrm((L, N_EXPERTS, D_EXPERT, D_MODEL), D_EXPERT ** -0.5),
        "ple_norm_g": 1.0 + nrm((L, D_MODEL), 0.05),
        "w_ple_in": nrm((L, PLE_DIM, D_MODEL), PLE_DIM ** -0.5),
        "w_ple_gate": nrm((L, D_MODEL, D_MODEL), D_MODEL ** -0.5),
    }


def reference(x, p, mix_norm_g, w_in, pool_w, pool_scale, sgu_norm_g, sgu_w, sgu_b,
              q_norm_g, k_norm_g, rel_bias, gate_b, w_branch_a, w_branch_b, w_branch_c,
              w_out, ffn_norm_g, w_group_router, b_group_router, w_expert_router,
              b_expert_router, w_expert_in, w_expert_out, ple_norm_g, w_ple_in, w_ple_gate):
    B, S, D = x.shape
    cuts = list(np.cumsum(SPLITS))
    for l in range(DEPTH):
        h = rms_norm(x, mix_norm_g[l])
        z = h @ w_in[l]
        a, u, v, q, k, vv, g = jnp.split(z, cuts, axis=-1)
        y_a = pool_mixer(a, pool_w[l], pool_scale[l])
        y_b = sgu_mixer(jax.nn.gelu(u), jax.nn.gelu(v), sgu_norm_g[l], sgu_w[l], sgu_b[l])
        y_c = chunk_attention(q.reshape(B, S, ATT_HEADS, HEAD_DIM),
                              k.reshape(B, S, ATT_HEADS, HEAD_DIM),
                              vv.reshape(B, S, ATT_HEADS, HEAD_DIM),
                              q_norm_g[l], k_norm_g[l], rel_bias[l])
        gates = jax.nn.sigmoid(g.reshape(B, S, N_BRANCH, D) + gate_b[l])
        merged = (gates[:, :, 0] * (y_a @ w_branch_a[l])
                  + gates[:, :, 1] * (y_b @ w_branch_b[l])
                  + gates[:, :, 2] * (y_c @ w_branch_c[l]))
        x = x + merged @ w_out[l]
        x = x + hier_moe(rms_norm(x, ffn_norm_g[l]), w_group_router[l], b_group_router[l],
                         w_expert_router[l], b_expert_router[l], w_expert_in[l], w_expert_out[l])
        x = x + (p[l] @ w_ple_in[l]) * jax.nn.sigmoid(rms_norm(x, ple_norm_g[l]) @ w_ple_gate[l])
    return x
```

```python
import functools

import jax
import jax.numpy as jnp
import numpy as np
from jax import lax
from jax.experimental import pallas as pl
from jax.experimental.pallas import tpu as pltpu

F32 = jnp.float32
BF16 = jnp.bfloat16
U32 = jnp.uint32
I32 = jnp.int32

D_MODEL = 1024
CHUNK = 64
PLE_DIM = 256
EPS = 1e-6
POOL_WINDOWS = (2, 4, 8, 16)
POOL_WIDTH = 256
POOL_GROUP_DIM = 64
SGU_BLOCK = 128
SGU_WIDTH = 256
SGU_GROUPS = 4
SGU_GROUP_DIM = 64
HEAD_DIM = 64
ATT_WIDTH = 512
ATT_HEADS = 8
LEFT_CHUNKS = 8
REL_CLIP = 128
N_BRANCH = 3
N_GROUPS = 4
EXPERTS_PER_GROUP = 8
D_EXPERT = 128
GROUP_FF = EXPERTS_PER_GROUP * D_EXPERT

C_A, C_U, C_V, C_Q, C_K, C_VV, C_G = 0, 256, 512, 768, 1280, 1792, 2304
IN_WIDTH = C_G + N_BRANCH * D_MODEL

LANES = 128
SUBLANES = 8
VMEM_LIMIT = 56 * 1024 * 1024

ROWS_PROJ = 512
ROWS_MIX = 256
HALO_BLOCKS = 2
POOL_HALO = 16
MOE_CHUNK = 2048
MOE_TILE = 256
PACK_W = D_MODEL // 2
ROW_W = PACK_W + LANES
NEG = -1e30


def _const_spec(shape):
    nd = len(shape)
    return pl.BlockSpec(shape, lambda *_: (0,) * nd, pipeline_mode=pl.Buffered(1))


def _params(n_axes):
    return pltpu.CompilerParams(dimension_semantics=("arbitrary",) * n_axes,
                                vmem_limit_bytes=VMEM_LIMIT)


def _rms(x, g):
    ms = jnp.mean(x * x, axis=-1, keepdims=True)
    return x * lax.rsqrt(ms + EPS) * g


def _gelu_tanh(x):
    c = np.float32(np.sqrt(2.0 / np.pi))
    return 0.5 * x * (1.0 + jnp.tanh(c * (x + np.float32(0.044715) * (x * x * x))))


def _sigmoid(x):
    return 0.5 * jnp.tanh(0.5 * x) + 0.5


def _dot(a, b):
    return jnp.dot(a, b, preferred_element_type=F32)


def _dot_nt(a, b):
    return lax.dot_general(a, b, (((1,), (1,)), ((), ())), preferred_element_type=F32)


def _pack_pairs(x):
    w = x.shape[1] // 2
    bits = pltpu.bitcast(x.astype(BF16).astype(F32), U32)
    return (bits[:, :w] >> 16) | (bits[:, w:] & np.uint32(0xFFFF0000))


def _unpack_pairs(p):
    lo = pltpu.bitcast(p << 16, F32)
    hi = pltpu.bitcast(p & np.uint32(0xFFFF0000), F32)
    return jnp.concatenate([lo, hi], axis=1)


def _inproj_kernel(x_ref, g_ref, w_ref, sgn_ref, gq_ref, gk_ref, hm_ref, gb_ref,
                   a_ref, u_ref, v_ref, q_ref, k_ref, vv_ref, gates_ref):
    h = _rms(x_ref[...], g_ref[...]).astype(BF16)

    def seg(c0, c1):
        return _dot(h, w_ref[:, c0:c1])

    def head_norm(z, g):
        ms = _dot((z * z).astype(BF16), hm_ref[...])
        return z * lax.rsqrt(ms + EPS) * g

    a_ref[...] = seg(C_A, C_U).astype(BF16)
    u_ref[...] = _gelu_tanh(seg(C_U, C_V)).astype(BF16)
    v_ref[...] = _rms(_gelu_tanh(seg(C_V, C_Q)), sgn_ref[...]).astype(BF16)
    q_ref[...] = (head_norm(seg(C_Q, C_K), gq_ref[...]) * np.float32(HEAD_DIM ** -0.5)).astype(BF16)
    k_ref[...] = head_norm(seg(C_K, C_VV), gk_ref[...]).astype(BF16)
    vv_ref[...] = seg(C_VV, C_G).astype(BF16)
    step = 512
    for c0 in range(0, N_BRANCH * D_MODEL, step):
        z = seg(C_G + c0, C_G + c0 + step) + gb_ref[:, c0:c0 + step]
        gates_ref[:, c0:c0 + step] = _sigmoid(z).astype(BF16)


def _in_proj(x2d, g, w_in, sgn, gq, gk, hm, gb):
    t = x2d.shape[0]
    tm = ROWS_PROJ
    row = lambda w: pl.BlockSpec((tm, w), lambda i: (i, 0))
    widths = (POOL_WIDTH, SGU_WIDTH, SGU_WIDTH, ATT_WIDTH, ATT_WIDTH, ATT_WIDTH, N_BRANCH * D_MODEL)
    return pl.pallas_call(
        _inproj_kernel,
        grid=(t // tm,),
        in_specs=[row(D_MODEL), _const_spec(g.shape), _const_spec(w_in.shape), _const_spec(sgn.shape),
                  _const_spec(gq.shape), _const_spec(gk.shape), _const_spec(hm.shape), _const_spec(gb.shape)],
        out_specs=[row(w) for w in widths],
        out_shape=[jax.ShapeDtypeStruct((t, w), BF16) for w in widths],
        compiler_params=_params(1),
        name="in_proj",
    )(x2d, g, w_in, sgn, gq, gk, hm, gb)


def _mixer_kernel(blocks_per_seq,
                  x_ref, a_ref, ap_ref, u_ref, v_ref, q_ref,
                  k0_ref, k1_ref, k2_ref, v0_ref, v1_ref, v2_ref, gates_ref,
                  invf_ref, invr_ref, bd_ref, ps_ref, swc_ref, sb_ref, bias_ref,
                  wa_ref, wb_ref, wc_ref, wo_ref,
                  o_ref, pool_buf):
    r = ROWS_MIX
    bi = pl.program_id(0) % blocks_per_seq
    first = bi == 0

    a = a_ref[...].astype(F32)
    halo = jnp.where(first, 0.0, ap_ref[r - POOL_HALO:, :].astype(F32))
    pool_buf[0:POOL_HALO, :] = jnp.zeros((POOL_HALO, POOL_WIDTH), F32)
    pool_buf[POOL_HALO:2 * POOL_HALO, :] = halo
    pool_buf[2 * POOL_HALO:, :] = a
    lane_a = lax.broadcasted_iota(I32, (r, POOL_WIDTH), 1) // POOL_GROUP_DIM
    wsum = None
    for gi, k in enumerate((1, 2, 4, 8)):
        cur = pool_buf[POOL_HALO:, :] + pool_buf[POOL_HALO - k:r + 2 * POOL_HALO - k, :]
        pool_buf[POOL_HALO:, :] = cur
        level = cur[POOL_HALO:, :]
        wsum = level if wsum is None else jnp.where(lane_a >= gi, level, wsum)
    inv = jnp.where(first, invf_ref[...], invr_ref[...])
    pooled = (wsum * inv - a).astype(BF16)
    ya = (_dot(pooled, bd_ref[...]) * ps_ref[...]).astype(BF16)

    vb = v_ref[...]
    lane_b = lax.broadcasted_iota(I32, (SGU_BLOCK, SGU_WIDTH), 1) // SGU_GROUP_DIM
    zero_b = jnp.zeros((SGU_BLOCK, SGU_WIDTH), BF16)
    mixed = []
    for s in range(r // SGU_BLOCK):
        blk = vb[s * SGU_BLOCK:(s + 1) * SGU_BLOCK, :]
        rhs = jnp.concatenate([jnp.where(lane_b == g, blk, zero_b) for g in range(SGU_GROUPS)], axis=0)
        mixed.append(_dot(swc_ref[...], rhs) + sb_ref[...])
    yb = (u_ref[...].astype(F32) * jnp.concatenate(mixed, axis=0)).astype(BF16)

    lane_c = lax.broadcasted_iota(I32, (r, LANES), 1)
    zero_q = jnp.zeros((r, LANES), BF16)
    k_refs = (k0_ref, k1_ref, k2_ref)
    v_refs = (v0_ref, v1_ref, v2_ref)
    blk_ok = (bi >= 2, bi >= 1, None)
    yc = []
    for pr in range(ATT_HEADS // 2):
        cols = slice(pr * LANES, (pr + 1) * LANES)
        qp = q_ref[:, cols]
        kps = [kr[:, cols] for kr in k_refs]
        vps = [vr[:, cols] for vr in v_refs]
        outs = []
        for s in range(2):
            hd = 2 * pr + s
            keep = (lane_c < HEAD_DIM) if s == 0 else (lane_c >= HEAD_DIM)
            qm = jnp.where(keep, qp, zero_q)
            parts = []
            for j in range(HALO_BLOCKS + 1):
                bias = bias_ref[hd, :, j * r:(j + 1) * r]
                if blk_ok[j] is not None:
                    bias = jnp.where(blk_ok[j], bias, NEG)
                parts.append(_dot_nt(qm, kps[j]) + bias)
            sc = jnp.concatenate(parts, axis=1)
            m = jnp.max(sc, axis=-1, keepdims=True)
            p = jnp.exp(sc - m)
            denom = jnp.sum(p, axis=-1, keepdims=True)
            pb = p.astype(BF16)
            o = _dot(pb[:, 0:r], vps[0])
            for j in range(1, HALO_BLOCKS + 1):
                o = o + _dot(pb[:, j * r:(j + 1) * r], vps[j])
            outs.append(o / denom)
        yc.append(jnp.where(lane_c < HEAD_DIM, outs[0], outs[1]))
    yc = jnp.concatenate(yc, axis=1).astype(BF16)

    d = D_MODEL
    merged = (gates_ref[:, 0:d].astype(F32) * _dot(ya, wa_ref[...])
              + gates_ref[:, d:2 * d].astype(F32) * _dot(yb, wb_ref[...])
              + gates_ref[:, 2 * d:3 * d].astype(F32) * _dot(yc, wc_ref[...]))
    o_ref[...] = x_ref[...] + _dot(merged.astype(BF16), wo_ref[...])


def _mixer(x2d, a, u, v, q, k, vv, gates, consts, seq):
    t = x2d.shape[0]
    r = ROWS_MIX
    row = lambda w: pl.BlockSpec((r, w), lambda i: (i, 0))
    prev = lambda w, n: pl.BlockSpec((r, w), lambda i: (jnp.maximum(i - n, 0), 0))
    in_specs = [row(D_MODEL), row(POOL_WIDTH), prev(POOL_WIDTH, 1), row(SGU_WIDTH), row(SGU_WIDTH),
                row(ATT_WIDTH),
                prev(ATT_WIDTH, 2), prev(ATT_WIDTH, 1), row(ATT_WIDTH),
                prev(ATT_WIDTH, 2), prev(ATT_WIDTH, 1), row(ATT_WIDTH),
                row(N_BRANCH * D_MODEL)] + [_const_spec(c.shape) for c in consts]
    return pl.pallas_call(
        functools.partial(_mixer_kernel, seq // r),
        grid=(t // r,),
        in_specs=in_specs,
        out_specs=row(D_MODEL),
        out_shape=jax.ShapeDtypeStruct((t, D_MODEL), F32),
        scratch_shapes=[pltpu.VMEM((r + 2 * POOL_HALO, POOL_WIDTH), F32)],
        compiler_params=_params(1),
        name="mixer",
    )(x2d, a, a, u, v, q, k, k, k, vv, vv, vv, gates, *consts)


ROUTER_ROWS = SUBLANES + N_GROUPS * EXPERTS_PER_GROUP


def _router_kernel(x_ref, g_ref, wr_ref, br_ref, hp_ref, gid_ref):
    tm = x_ref.shape[0]
    h = _rms(x_ref[...], g_ref[...])
    h_hi = h.astype(BF16)
    h_lo = (h - h_hi.astype(F32)).astype(BF16)
    nr = ROUTER_ROWS
    l1 = _dot_nt(wr_ref[...], h_hi)
    l2 = _dot_nt(wr_ref[0:nr, :], h_lo)
    logits = l1[0:nr] + l1[nr:2 * nr] + l2 + br_ref[...]

    rows = lax.broadcasted_iota(I32, (SUBLANES, tm), 0).astype(F32)
    big = np.float32(1 << 20)
    gl = jnp.where(rows < N_GROUPS, logits[0:SUBLANES], NEG)
    gmax = jnp.max(gl, axis=0, keepdims=True)
    g_p = 1.0 / jnp.sum(jnp.exp(gl - gmax), axis=0, keepdims=True)
    g_idx = jnp.min(jnp.where(gl == gmax, rows, big), axis=0, keepdims=True)

    el = logits[SUBLANES:2 * SUBLANES]
    for gi in range(1, N_GROUPS):
        el = jnp.where(g_idx == gi, logits[SUBLANES * (gi + 1):SUBLANES * (gi + 2)], el)
    emax = jnp.max(el, axis=0, keepdims=True)
    esum = jnp.sum(jnp.exp(el - emax), axis=0, keepdims=True)
    i0 = jnp.min(jnp.where(el == emax, rows, big), axis=0, keepdims=True)
    el2 = jnp.where(rows == i0, -jnp.inf, el)
    m2 = jnp.max(el2, axis=0, keepdims=True)
    i1 = jnp.min(jnp.where(el2 == m2, rows, big), axis=0, keepdims=True)
    p0 = 1.0 / esum
    p1 = jnp.exp(m2 - emax) / esum
    w0 = g_p * p0 / (p0 + p1)
    w1 = g_p * p1 / (p0 + p1)
    gates = jnp.where(rows == i0, w0, 0.0) + jnp.where(rows == i1, w1, 0.0)

    gates_t = jnp.concatenate([gates, jnp.zeros((LANES - SUBLANES, tm), F32)], axis=0).T
    hp_ref[:, 0:PACK_W] = _pack_pairs(h)
    hp_ref[:, PACK_W:ROW_W] = pltpu.bitcast(gates_t, U32)
    gid_ref[...] = g_idx.astype(I32).reshape(1, 1, tm)


def _router(x2d, g, wr, br):
    t = x2d.shape[0]
    tm = ROWS_PROJ
    return pl.pallas_call(
        _router_kernel,
        grid=(t // tm,),
        in_specs=[pl.BlockSpec((tm, D_MODEL), lambda i: (i, 0)), _const_spec(g.shape),
                  _const_spec(wr.shape), _const_spec(br.shape)],
        out_specs=[pl.BlockSpec((tm, ROW_W), lambda i: (i, 0)),
                   pl.BlockSpec((1, 1, tm), lambda i: (i, 0, 0))],
        out_shape=[jax.ShapeDtypeStruct((t, ROW_W), U32),
                   jax.ShapeDtypeStruct((t // tm, 1, tm), I32)],
        compiler_params=_params(1),
        name="router",
    )(x2d, g, wr, br)


def _moe_kernel(chunk, sidx_ref, offs_ref, cnts_ref, hp_ref, win_ref, wout_ref, o_ref, xg_ref, yp_ref):
    c = pl.program_id(0)
    g = pl.program_id(1)
    off = offs_ref[c * N_GROUPS + g]
    n = cnts_ref[c * N_GROUPS + g]
    base = c * chunk
    last = off + n - 1
    tr = MOE_TILE
    unroll = SUBLANES

    def tile_body(t, carry):
        r0 = off + t * tr

        def token(rr):
            return sidx_ref[base + jnp.minimum(r0 + rr, last)]

        def gather(i, cr):
            for uu in range(unroll):
                rr = i * unroll + uu
                xg_ref[pl.ds(rr, 1), :] = hp_ref[pl.ds(token(rr), 1), :]
            return cr

        lax.fori_loop(0, tr // unroll, gather, 0)

        xb = _unpack_pairs(xg_ref[:, 0:PACK_W]).astype(BF16)
        gates = pltpu.bitcast(xg_ref[:, PACK_W:ROW_W], F32)
        w_in = win_ref.at[0]
        hg = _dot(xb, w_in[:, 0:GROUP_FF])
        hu = _dot(xb, w_in[:, GROUP_FF:2 * GROUP_FF])
        act = hg * _sigmoid(hg) * hu
        act = jnp.concatenate(
            [act[:, e * D_EXPERT:(e + 1) * D_EXPERT] * gates[:, e:e + 1] for e in range(EXPERTS_PER_GROUP)],
            axis=1).astype(BF16)
        yp_ref[...] = _pack_pairs(_dot(act, wout_ref.at[0][...]))

        def scatter(i, cr):
            for uu in range(unroll):
                rr = i * unroll + uu
                o_ref[pl.ds(token(rr), 1), :] = yp_ref[pl.ds(rr, 1), :]
            return cr

        lax.fori_loop(0, tr // unroll, scatter, 0)
        return carry

    lax.fori_loop(0, (n + tr - 1) // tr, tile_body, 0)


def _moe(hp, sidx, offs, cnts, win, wout, chunk):
    t = hp.shape[0]
    grid_spec = pltpu.PrefetchScalarGridSpec(
        num_scalar_prefetch=3,
        grid=(t // chunk, N_GROUPS),
        in_specs=[pl.BlockSpec((chunk, ROW_W), lambda c, g, *_: (c, 0)),
                  pl.BlockSpec((1, D_MODEL, 2 * GROUP_FF), lambda c, g, *_: (g, 0, 0)),
                  pl.BlockSpec((1, GROUP_FF, D_MODEL), lambda c, g, *_: (g, 0, 0))],
        out_specs=pl.BlockSpec((chunk, PACK_W), lambda c, g, *_: (c, 0)),
        scratch_shapes=[pltpu.VMEM((MOE_TILE, ROW_W), U32), pltpu.VMEM((MOE_TILE, PACK_W), U32)],
    )
    return pl.pallas_call(
        functools.partial(_moe_kernel, chunk),
        grid_spec=grid_spec,
        out_shape=jax.ShapeDtypeStruct((t, PACK_W), U32),
        compiler_params=_params(2),
        name="moe",
    )(sidx, offs, cnts, hp, win, wout)


def _ple_kernel(x_ref, mo_ref, p_ref, g_ref, wpi_ref, wg_ref, o_ref):
    x = x_ref[...] + _unpack_pairs(mo_ref[...])
    hn = _rms(x, g_ref[...]).astype(BF16)
    gate = _sigmoid(_dot(hn, wg_ref[...]))
    emb = _dot(p_ref[...].astype(BF16), wpi_ref[...])
    o_ref[...] = x + emb * gate


def _ple(x2d, mo, p2d, g, wpi, wg):
    t = x2d.shape[0]
    tm = ROWS_PROJ
    row = lambda w: pl.BlockSpec((tm, w), lambda i: (i, 0))
    return pl.pallas_call(
        _ple_kernel,
        grid=(t // tm,),
        in_specs=[row(D_MODEL), row(PACK_W), row(PLE_DIM), _const_spec(g.shape),
                  _const_spec(wpi.shape), _const_spec(wg.shape)],
        out_specs=row(D_MODEL),
        out_shape=jax.ShapeDtypeStruct((t, D_MODEL), F32),
        compiler_params=_params(1),
        name="ple",
    )(x2d, mo, p2d, g, wpi, wg)


def _attention_bias(rel_bias):
    r = ROWS_MIX
    qi = np.arange(r)[:, None]
    kj = np.arange((HALO_BLOCKS + 1) * r)[None, :]
    dist = qi + HALO_BLOCKS * r - kj
    rel = np.clip(dist, -REL_CLIP, REL_CLIP) + REL_CLIP
    qc = qi // CHUNK
    kc = kj // CHUNK
    valid = (kc >= qc) & (kc <= qc + LEFT_CHUNKS)
    return jnp.where(jnp.asarray(valid)[None], rel_bias[:, rel].astype(F32), NEG)


def _pool_inverse_counts():
    r = ROWS_MIX
    win = np.repeat(np.asarray(POOL_WINDOWS, np.float32), POOL_GROUP_DIM)[None, :]
    pos = np.arange(1, r + 1, dtype=np.float32)[:, None]
    return jnp.asarray(1.0 / np.minimum(pos, win), F32), jnp.asarray(1.0 / win, F32)


def _block_diag(w):
    g, c, d = w.shape
    out = jnp.zeros((g * c, g * d), w.dtype)
    for i in range(g):
        out = out.at[i * c:(i + 1) * c, i * d:(i + 1) * d].set(w[i])
    return out


def _layer_consts(l, prm):
    row = lambda v: v.reshape(1, -1).astype(F32)
    hm = _block_diag(jnp.full((ATT_HEADS, HEAD_DIM, HEAD_DIM), 1.0 / HEAD_DIM, F32)).astype(BF16)
    inproj = dict(
        g=row(prm["mix_norm_g"][l]), w_in=prm["w_in"][l].astype(BF16), sgn=row(prm["sgu_norm_g"][l]),
        gq=row(jnp.tile(prm["q_norm_g"][l], ATT_HEADS)), gk=row(jnp.tile(prm["k_norm_g"][l], ATT_HEADS)),
        hm=hm, gb=row(prm["gate_b"][l]))

    i = np.arange(SGU_BLOCK)
    tri = (i[None, :] // CHUNK) <= (i[:, None] // CHUNK)
    sw = jnp.where(jnp.asarray(tri)[None], prm["sgu_w"][l], 0.0)
    swc = jnp.concatenate([sw[g] for g in range(SGU_GROUPS)], axis=1).astype(BF16)
    sbm = jnp.repeat(prm["sgu_b"][l].T, SGU_GROUP_DIM, axis=1).astype(F32)
    invf, invr = _pool_inverse_counts()
    mixer = [invf, invr, _block_diag(prm["pool_w"][l]).astype(BF16), row(prm["pool_scale"][l]),
             swc, sbm, _attention_bias(prm["rel_bias"][l]),
             prm["w_branch_a"][l].astype(BF16), prm["w_branch_b"][l].astype(BF16),
             prm["w_branch_c"][l].astype(BF16), prm["w_out"][l].astype(BF16)]

    wr = jnp.concatenate([prm["w_group_router"][l].T, jnp.zeros((SUBLANES - N_GROUPS, D_MODEL), F32),
                          prm["w_expert_router"][l].T], axis=0)
    wr_hi = wr.astype(BF16)
    wr_lo = (wr - wr_hi.astype(F32)).astype(BF16)
    br = jnp.concatenate([prm["b_group_router"][l], jnp.zeros((SUBLANES - N_GROUPS,), F32),
                          prm["b_expert_router"][l]]).reshape(-1, 1).astype(F32)
    router = dict(g=row(prm["ffn_norm_g"][l]), wr=jnp.concatenate([wr_hi, wr_lo], axis=0), br=br)

    wi = prm["w_expert_in"][l].reshape(N_GROUPS, EXPERTS_PER_GROUP, D_MODEL, 2, D_EXPERT)
    wi = jnp.transpose(wi, (0, 2, 3, 1, 4)).reshape(N_GROUPS, D_MODEL, 2 * GROUP_FF).astype(BF16)
    wo = prm["w_expert_out"][l].reshape(N_GROUPS, GROUP_FF, D_MODEL).astype(BF16)
    moe = dict(win=wi, wout=wo)

    ple = dict(g=row(prm["ple_norm_g"][l]), wpi=prm["w_ple_in"][l].astype(BF16),
               wg=prm["w_ple_gate"][l].astype(BF16))
    return inproj, mixer, router, moe, ple


def _sort_by_group(gid, chunk):
    gid = gid.reshape(-1, chunk)
    sidx = jnp.argsort(gid, axis=1, stable=True).astype(I32)
    cnts = jnp.sum(gid[:, :, None] == jnp.arange(N_GROUPS, dtype=I32)[None, None, :], axis=1).astype(I32)
    offs = jnp.cumsum(cnts, axis=1) - cnts
    return sidx.reshape(-1), offs.reshape(-1).astype(I32), cnts.reshape(-1)


def kernel(x, p, mix_norm_g, w_in, pool_w, pool_scale, sgu_norm_g, sgu_w, sgu_b, q_norm_g, k_norm_g, rel_bias, gate_b, w_branch_a, w_branch_b, w_branch_c, w_out, ffn_norm_g, w_group_router, b_group_router, w_expert_router, b_expert_router, w_expert_in, w_expert_out, ple_norm_g, w_ple_in, w_ple_gate):
    prm = dict(mix_norm_g=mix_norm_g, w_in=w_in, pool_w=pool_w, pool_scale=pool_scale, sgu_norm_g=sgu_norm_g,
               sgu_w=sgu_w, sgu_b=sgu_b, q_norm_g=q_norm_g, k_norm_g=k_norm_g, rel_bias=rel_bias, gate_b=gate_b,
               w_branch_a=w_branch_a, w_branch_b=w_branch_b, w_branch_c=w_branch_c, w_out=w_out,
               ffn_norm_g=ffn_norm_g, w_group_router=w_group_router, b_group_router=b_group_router,
               w_expert_router=w_expert_router, b_expert_router=b_expert_router, w_expert_in=w_expert_in,
               w_expert_out=w_expert_out, ple_norm_g=ple_norm_g, w_ple_in=w_ple_in, w_ple_gate=w_ple_gate)
    b, s, d = x.shape
    t = b * s
    depth = p.shape[0]
    chunk = min(MOE_CHUNK, t)
    assert d == D_MODEL and s % ROWS_MIX == 0 and t % ROWS_PROJ == 0 and t % chunk == 0
    assert s // ROWS_MIX >= HALO_BLOCKS + 1
    xf = x.reshape(t, d)
    for l in range(depth):
        c_in, c_mix, c_rt, c_moe, c_ple = _layer_consts(l, prm)
        a, u, v, q, k, vv, gates = _in_proj(xf, **c_in)
        xf = _mixer(xf, a, u, v, q, k, vv, gates, c_mix, s)
        hp, gid = _router(xf, **c_rt)
        sidx, offs, cnts = _sort_by_group(gid, chunk)
        mo = _moe(hp, sidx, offs, cnts, c_moe["win"], c_moe["wout"], chunk)
        xf = _ple(xf, mo, p[l].reshape(t, PLE_DIM), **c_ple)
    return xf.reshape(b, s, d)
```

```python
import functools

import jax
import jax.numpy as jnp
import numpy as np
from jax import lax
from jax.experimental import pallas as pl
from jax.experimental.pallas import tpu as pltpu

F32 = jnp.float32
BF16 = jnp.bfloat16
U32 = jnp.uint32
I32 = jnp.int32

D_MODEL = 1024
CHUNK = 64
PLE_DIM = 256
EPS = 1e-6
POOL_WINDOWS = (2, 4, 8, 16)
POOL_WIDTH = 256
POOL_GROUP_DIM = 64
SGU_BLOCK = 128
SGU_WIDTH = 256
SGU_GROUPS = 4
SGU_GROUP_DIM = 64
HEAD_DIM = 64
ATT_WIDTH = 512
ATT_HEADS = 8
LEFT_CHUNKS = 8
REL_CLIP = 128
N_BRANCH = 3
N_GROUPS = 4
EXPERTS_PER_GROUP = 8
D_EXPERT = 128
GROUP_FF = EXPERTS_PER_GROUP * D_EXPERT

C_A, C_U, C_V, C_Q, C_K, C_VV, C_G = 0, 256, 512, 768, 1280, 1792, 2304
IN_WIDTH = C_G + N_BRANCH * D_MODEL

LANES = 128
SUBLANES = 8
VMEM_LIMIT = 56 * 1024 * 1024

ROWS_PROJ = 512
ROWS_MIX = 256
HALO_BLOCKS = 2
POOL_HALO = 16
MOE_CHUNK = 2048
MOE_TILE = 256
PACK_W = D_MODEL // 2
ROW_W = PACK_W + LANES
NEG = -1e30


def _const_spec(shape):
    nd = len(shape)
    return pl.BlockSpec(shape, lambda *_: (0,) * nd, pipeline_mode=pl.Buffered(1))


def _params(n_axes):
    return pltpu.CompilerParams(dimension_semantics=("arbitrary",) * n_axes,
                                vmem_limit_bytes=VMEM_LIMIT)


def _rms(x, g):
    ms = jnp.mean(x * x, axis=-1, keepdims=True)
    return x * lax.rsqrt(ms + EPS) * g


def _gelu_tanh(x):
    c = np.float32(np.sqrt(2.0 / np.pi))
    return 0.5 * x * (1.0 + jnp.tanh(c * (x + np.float32(0.044715) * (x * x * x))))


def _sigmoid(x):
    return 0.5 * jnp.tanh(0.5 * x) + 0.5


def _dot(a, b):
    return jnp.dot(a, b, preferred_element_type=F32)


def _dot_nt(a, b):
    return lax.dot_general(a, b, (((1,), (1,)), ((), ())), preferred_element_type=F32)


def _pack_pairs(x):
    w = x.shape[1] // 2
    bits = pltpu.bitcast(x.astype(BF16).astype(F32), U32)
    return (bits[:, :w] >> 16) | (bits[:, w:] & np.uint32(0xFFFF0000))


def _unpack_pairs(p):
    lo = pltpu.bitcast(p << 16, F32)
    hi = pltpu.bitcast(p & np.uint32(0xFFFF0000), F32)
    return jnp.concatenate([lo, hi], axis=1)


def _inproj_kernel(x_ref, g_ref, w_ref, sgn_ref, gq_ref, gk_ref, hm_ref, gb_ref,
                   a_ref, u_ref, v_ref, q_ref, k_ref, vv_ref, gates_ref):
    h = _rms(x_ref[...], g_ref[...]).astype(BF16)

    def seg(c0, c1):
        return _dot(h, w_ref[:, c0:c1])

    def head_norm(z, g):
        ms = _dot((z * z).astype(BF16), hm_ref[...])
        return z * lax.rsqrt(ms + EPS) * g

    a_ref[...] = seg(C_A, C_U).astype(BF16)
    u_ref[...] = _gelu_tanh(seg(C_U, C_V)).astype(BF16)
    v_ref[...] = _rms(_gelu_tanh(seg(C_V, C_Q)), sgn_ref[...]).astype(BF16)
    q_ref[...] = (head_norm(seg(C_Q, C_K), gq_ref[...]) * np.float32(HEAD_DIM ** -0.5)).astype(BF16)
    k_ref[...] = head_norm(seg(C_K, C_VV), gk_ref[...]).astype(BF16)
    vv_ref[...] = seg(C_VV, C_G).astype(BF16)
    step = 512
    for c0 in range(0, N_BRANCH * D_MODEL, step):
        z = seg(C_G + c0, C_G + c0 + step) + gb_ref[:, c0:c0 + step]
        gates_ref[:, c0:c0 + step] = _sigmoid(z).astype(BF16)


def _in_proj(x2d, g, w_in, sgn, gq, gk, hm, gb):
    t = x2d.shape[0]
    tm = ROWS_PROJ
    row = lambda w: pl.BlockSpec((tm, w), lambda i: (i, 0))
    widths = (POOL_WIDTH, SGU_WIDTH, SGU_WIDTH, ATT_WIDTH, ATT_WIDTH, ATT_WIDTH, N_BRANCH * D_MODEL)
    return pl.pallas_call(
        _inproj_kernel,
        grid=(t // tm,),
        in_specs=[row(D_MODEL), _const_spec(g.shape), _const_spec(w_in.shape), _const_spec(sgn.shape),
                  _const_spec(gq.shape), _const_spec(gk.shape), _const_spec(hm.shape), _const_spec(gb.shape)],
        out_specs=[row(w) for w in widths],
        out_shape=[jax.ShapeDtypeStruct((t, w), BF16) for w in widths],
        compiler_params=_params(1),
        name="in_proj",
    )(x2d, g, w_in, sgn, gq, gk, hm, gb)


def _mixer_kernel(blocks_per_seq,
                  x_ref, a_ref, ap_ref, u_ref, v_ref, q_ref,
                  k0_ref, k1_ref, k2_ref, v0_ref, v1_ref, v2_ref, gates_ref,
                  invf_ref, invr_ref, bd_ref, ps_ref, swc_ref, sb_ref, bias_ref,
                  wa_ref, wb_ref, wc_ref, wo_ref,
                  o_ref, pool_buf):
    r = ROWS_MIX
    bi = pl.program_id(0) % blocks_per_seq
    first = bi == 0

    a = a_ref[...].astype(F32)
    halo = jnp.where(first, 0.0, ap_ref[r - POOL_HALO:, :].astype(F32))
    pool_buf[0:POOL_HALO, :] = jnp.zeros((POOL_HALO, POOL_WIDTH), F32)
    pool_buf[POOL_HALO:2 * POOL_HALO, :] = halo
    pool_buf[2 * POOL_HALO:, :] = a
    lane_a = lax.broadcasted_iota(I32, (r, POOL_WIDTH), 1) // POOL_GROUP_DIM
    wsum = None
    for gi, k in enumerate((1, 2, 4, 8)):
        cur = pool_buf[POOL_HALO:, :] + pool_buf[POOL_HALO - k:r + 2 * POOL_HALO - k, :]
        pool_buf[POOL_HALO:, :] = cur
        level = cur[POOL_HALO:, :]
        wsum = level if wsum is None else jnp.where(lane_a >= gi, level, wsum)
    inv = jnp.where(first, invf_ref[...], invr_ref[...])
    pooled = (wsum * inv - a).astype(BF16)
    ya = (_dot(pooled, bd_ref[...]) * ps_ref[...]).astype(BF16)

    vb = v_ref[...]
    lane_b = lax.broadcasted_iota(I32, (SGU_BLOCK, SGU_WIDTH), 1) // SGU_GROUP_DIM
    zero_b = jnp.zeros((SGU_BLOCK, SGU_WIDTH), BF16)
    mixed = []
    for s in range(r // SGU_BLOCK):
        blk = vb[s * SGU_BLOCK:(s + 1) * SGU_BLOCK, :]
        rhs = jnp.concatenate([jnp.where(lane_b == g, blk, zero_b) for g in range(SGU_GROUPS)], axis=0)
        mixed.append(_dot(swc_ref[...], rhs) + sb_ref[...])
    yb = (u_ref[...].astype(F32) * jnp.concatenate(mixed, axis=0)).astype(BF16)

    lane_c = lax.broadcasted_iota(I32, (r, LANES), 1)
    zero_q = jnp.zeros((r, LANES), BF16)
    k_refs = (k0_ref, k1_ref, k2_ref)
    v_refs = (v0_ref, v1_ref, v2_ref)
    blk_ok = (bi >= 2, bi >= 1, None)
    yc = []
    for pr in range(ATT_HEADS // 2):
        cols = slice(pr * LANES, (pr + 1) * LANES)
        qp = q_ref[:, cols]
        kps = [kr[:, cols] for kr in k_refs]
        vps = [vr[:, cols] for vr in v_refs]
        outs = []
        for s in range(2):
            hd = 2 * pr + s
            keep = (lane_c < HEAD_DIM) if s == 0 else (lane_c >= HEAD_DIM)
            qm = jnp.where(keep, qp, zero_q)
            parts = []
            for j in range(HALO_BLOCKS + 1):
                bias = bias_ref[hd, :, j * r:(j + 1) * r]
                if blk_ok[j] is not None:
                    bias = jnp.where(blk_ok[j], bias, NEG)
                parts.append(_dot_nt(qm, kps[j]) + bias)
            sc = jnp.concatenate(parts, axis=1)
            m = jnp.max(sc, axis=-1, keepdims=True)
            p = jnp.exp(sc - m)
            denom = jnp.sum(p, axis=-1, keepdims=True)
            pb = p.astype(BF16)
            o = _dot(pb[:, 0:r], vps[0])
            for j in range(1, HALO_BLOCKS + 1):
                o = o + _dot(pb[:, j * r:(j + 1) * r], vps[j])
            outs.append(o / denom)
        yc.append(jnp.where(lane_c < HEAD_DIM, outs[0], outs[1]))
    yc = jnp.concatenate(yc, axis=1).astype(BF16)

    d = D_MODEL
    merged = (gates_ref[:, 0:d].astype(F32) * _dot(ya, wa_ref[...])
              + gates_ref[:, d:2 * d].astype(F32) * _dot(yb, wb_ref[...])
              + gates_ref[:, 2 * d:3 * d].astype(F32) * _dot(yc, wc_ref[...]))
    o_ref[...] = x_ref[...] + _dot(merged.astype(BF16), wo_ref[...])


def _mixer(x2d, a, u, v, q, k, vv, gates, consts, seq):
    t = x2d.shape[0]
    r = ROWS_MIX
    row = lambda w: pl.BlockSpec((r, w), lambda i: (i, 0))
    prev = lambda w, n: pl.BlockSpec((r, w), lambda i: (jnp.maximum(i - n, 0), 0))
    in_specs = [row(D_MODEL), row(POOL_WIDTH), prev(POOL_WIDTH, 1), row(SGU_WIDTH), row(SGU_WIDTH),
                row(ATT_WIDTH),
                prev(ATT_WIDTH, 2), prev(ATT_WIDTH, 1), row(ATT_WIDTH),
                prev(ATT_WIDTH, 2), prev(ATT_WIDTH, 1), row(ATT_WIDTH),
                row(N_BRANCH * D_MODEL)] + [_const_spec(c.shape) for c in consts]
    return pl.pallas_call(
        functools.partial(_mixer_kernel, seq // r),
        grid=(t // r,),
        in_specs=in_specs,
        out_specs=row(D_MODEL),
        out_shape=jax.ShapeDtypeStruct((t, D_MODEL), F32),
        scratch_shapes=[pltpu.VMEM((r + 2 * POOL_HALO, POOL_WIDTH), F32)],
        compiler_params=_params(1),
        name="mixer",
    )(x2d, a, a, u, v, q, k, k, k, vv, vv, vv, gates, *consts)


ROUTER_ROWS = SUBLANES + N_GROUPS * EXPERTS_PER_GROUP


def _router_kernel(x_ref, g_ref, wr_ref, br_ref, hp_ref, gid_ref):
    tm = x_ref.shape[0]
    h = _rms(x_ref[...], g_ref[...])
    h_hi = h.astype(BF16)
    h_lo = (h - h_hi.astype(F32)).astype(BF16)
    nr = ROUTER_ROWS
    l1 = _dot_nt(wr_ref[...], h_hi)
    l2 = _dot_nt(wr_ref[0:nr, :], h_lo)
    logits = l1[0:nr] + l1[nr:2 * nr] + l2 + br_ref[...]

    rows = lax.broadcasted_iota(I32, (SUBLANES, tm), 0).astype(F32)
    big = np.float32(1 << 20)
    gl = jnp.where(rows < N_GROUPS, logits[0:SUBLANES], NEG)
    gmax = jnp.max(gl, axis=0, keepdims=True)
    g_p = 1.0 / jnp.sum(jnp.exp(gl - gmax), axis=0, keepdims=True)
    g_idx = jnp.min(jnp.where(gl == gmax, rows, big), axis=0, keepdims=True)

    el = logits[SUBLANES:2 * SUBLANES]
    for gi in range(1, N_GROUPS):
        el = jnp.where(g_idx == gi, logits[SUBLANES * (gi + 1):SUBLANES * (gi + 2)], el)
    emax = jnp.max(el, axis=0, keepdims=True)
    esum = jnp.sum(jnp.exp(el - emax), axis=0, keepdims=True)
    i0 = jnp.min(jnp.where(el == emax, rows, big), axis=0, keepdims=True)
    el2 = jnp.where(rows == i0, -jnp.inf, el)
    m2 = jnp.max(el2, axis=0, keepdims=True)
    i1 = jnp.min(jnp.where(el2 == m2, rows, big), axis=0, keepdims=True)
    p0 = 1.0 / esum
    p1 = jnp.exp(m2 - emax) / esum
    w0 = g_p * p0 / (p0 + p1)
    w1 = g_p * p1 / (p0 + p1)
    gates = jnp.where(rows == i0, w0, 0.0) + jnp.where(rows == i1, w1, 0.0)

    gates_t = jnp.concatenate([gates, jnp.zeros((LANES - SUBLANES, tm), F32)], axis=0).T
    hp_ref[:, 0:PACK_W] = _pack_pairs(h)
    hp_ref[:, PACK_W:ROW_W] = pltpu.bitcast(gates_t, U32)
    gid_ref[...] = g_idx.astype(I32).reshape(1, 1, tm)


def _router(x2d, g, wr, br):
    t = x2d.shape[0]
    tm = ROWS_PROJ
    return pl.pallas_call(
        _router_kernel,
        grid=(t // tm,),
        in_specs=[pl.BlockSpec((tm, D_MODEL), lambda i: (i, 0)), _const_spec(g.shape),
                  _const_spec(wr.shape), _const_spec(br.shape)],
        out_specs=[pl.BlockSpec((tm, ROW_W), lambda i: (i, 0)),
                   pl.BlockSpec((1, 1, tm), lambda i: (i, 0, 0))],
        out_shape=[jax.ShapeDtypeStruct((t, ROW_W), U32),
                   jax.ShapeDtypeStruct((t // tm, 1, tm), I32)],
        compiler_params=_params(1),
        name="router",
    )(x2d, g, wr, br)


def _moe_kernel(chunk, sidx_ref, offs_ref, cnts_ref, hp_ref, win_ref, wout_ref, o_ref, xg_ref, yp_ref):
    c = pl.program_id(0)
    g = pl.program_id(1)
    off = offs_ref[c * N_GROUPS + g]
    n = cnts_ref[c * N_GROUPS + g]
    base = c * chunk
    last = off + n - 1
    tr = MOE_TILE
    unroll = SUBLANES

    def tile_body(t, carry):
        r0 = off + t * tr

        def token(rr):
            return sidx_ref[base + jnp.minimum(r0 + rr, last)]

        def gather(i, cr):
            for uu in range(unroll):
                rr = i * unroll + uu
                xg_ref[pl.ds(rr, 1), :] = hp_ref[pl.ds(token(rr), 1), :]
            return cr

        lax.fori_loop(0, tr // unroll, gather, 0)

        xb = _unpack_pairs(xg_ref[:, 0:PACK_W]).astype(BF16)
        gates = pltpu.bitcast(xg_ref[:, PACK_W:ROW_W], F32)
        w_in = win_ref.at[0]
        hg = _dot(xb, w_in[:, 0:GROUP_FF])
        hu = _dot(xb, w_in[:, GROUP_FF:2 * GROUP_FF])
        act = hg * _sigmoid(hg) * hu
        act = jnp.concatenate(
            [act[:, e * D_EXPERT:(e + 1) * D_EXPERT] * gates[:, e:e + 1] for e in range(EXPERTS_PER_GROUP)],
            axis=1).astype(BF16)
        yp_ref[...] = _pack_pairs(_dot(act, wout_ref.at[0][...]))

        def scatter(i, cr):
            for uu in range(unroll):
                rr = i * unroll + uu
                o_ref[pl.ds(token(rr), 1), :] = yp_ref[pl.ds(rr, 1), :]
            return cr

        lax.fori_loop(0, tr // unroll, scatter, 0)
        return carry

    lax.fori_loop(0, (n + tr - 1) // tr, tile_body, 0)


def _moe(hp, sidx, offs, cnts, win, wout, chunk):
    t = hp.shape[0]
    grid_spec = pltpu.PrefetchScalarGridSpec(
        num_scalar_prefetch=3,
        grid=(t // chunk, N_GROUPS),
        in_specs=[pl.BlockSpec((chunk, ROW_W), lambda c, g, *_: (c, 0)),
                  pl.BlockSpec((1, D_MODEL, 2 * GROUP_FF), lambda c, g, *_: (g, 0, 0)),
                  pl.BlockSpec((1, GROUP_FF, D_MODEL), lambda c, g, *_: (g, 0, 0))],
        out_specs=pl.BlockSpec((chunk, PACK_W), lambda c, g, *_: (c, 0)),
        scratch_shapes=[pltpu.VMEM((MOE_TILE, ROW_W), U32), pltpu.VMEM((MOE_TILE, PACK_W), U32)],
    )
    return pl.pallas_call(
        functools.partial(_moe_kernel, chunk),
        grid_spec=grid_spec,
        out_shape=jax.ShapeDtypeStruct((t, PACK_W), U32),
        compiler_params=_params(2),
        name="moe",
    )(sidx, offs, cnts, hp, win, wout)


def _ple_kernel(x_ref, mo_ref, p_ref, g_ref, wpi_ref, wg_ref, o_ref):
    x = x_ref[...] + _unpack_pairs(mo_ref[...])
    hn = _rms(x, g_ref[...]).astype(BF16)
    gate = _sigmoid(_dot(hn, wg_ref[...]))
    emb = _dot(p_ref[...].astype(BF16), wpi_ref[...])
    o_ref[...] = x + emb * gate


def _ple(x2d, mo, p2d, g, wpi, wg):
    t = x2d.shape[0]
    tm = ROWS_PROJ
    row = lambda w: pl.BlockSpec((tm, w), lambda i: (i, 0))
    return pl.pallas_call(
        _ple_kernel,
        grid=(t // tm,),
        in_specs=[row(D_MODEL), row(PACK_W), row(PLE_DIM), _const_spec(g.shape),
                  _const_spec(wpi.shape), _const_spec(wg.shape)],
        out_specs=row(D_MODEL),
        out_shape=jax.ShapeDtypeStruct((t, D_MODEL), F32),
        compiler_params=_params(1),
        name="ple",
    )(x2d, mo, p2d, g, wpi, wg)


def _attention_bias(rel_bias):
    r = ROWS_MIX
    nk = (HALO_BLOCKS + 1) * r
    qi = np.arange(r)[:, None]
    kj = np.arange(nk)[None, :]
    qc = qi // CHUNK
    kc = kj // CHUNK
    valid = (kc >= qc) & (kc <= qc + LEFT_CHUNKS)
    p = nk + r
    off = np.arange(p)
    off = np.where(off < nk, off, off - p)
    rel = np.clip(HALO_BLOCKS * r - off, -REL_CLIP, REL_CLIP) + REL_CLIP
    per_off = rel_bias[:, rel].astype(F32)
    h = per_off.shape[0]
    band = jnp.tile(per_off, (1, r))[:, :r * (p - 1)].reshape(h, r, p - 1)[:, :, :nk]
    return jnp.where(jnp.asarray(valid)[None], band, NEG)


def _pool_inverse_counts():
    r = ROWS_MIX
    win = np.repeat(np.asarray(POOL_WINDOWS, np.float32), POOL_GROUP_DIM)[None, :]
    pos = np.arange(1, r + 1, dtype=np.float32)[:, None]
    return jnp.asarray(1.0 / np.minimum(pos, win), F32), jnp.asarray(1.0 / win, F32)


def _block_diag(w):
    g, c, d = w.shape
    out = jnp.zeros((g * c, g * d), w.dtype)
    for i in range(g):
        out = out.at[i * c:(i + 1) * c, i * d:(i + 1) * d].set(w[i])
    return out


def _layer_consts(l, prm):
    row = lambda v: v.reshape(1, -1).astype(F32)
    hm = _block_diag(jnp.full((ATT_HEADS, HEAD_DIM, HEAD_DIM), 1.0 / HEAD_DIM, F32)).astype(BF16)
    inproj = dict(
        g=row(prm["mix_norm_g"][l]), w_in=prm["w_in"][l].astype(BF16), sgn=row(prm["sgu_norm_g"][l]),
        gq=row(jnp.tile(prm["q_norm_g"][l], ATT_HEADS)), gk=row(jnp.tile(prm["k_norm_g"][l], ATT_HEADS)),
        hm=hm, gb=row(prm["gate_b"][l]))

    i = np.arange(SGU_BLOCK)
    tri = (i[None, :] // CHUNK) <= (i[:, None] // CHUNK)
    sw = jnp.where(jnp.asarray(tri)[None], prm["sgu_w"][l], 0.0)
    swc = jnp.concatenate([sw[g] for g in range(SGU_GROUPS)], axis=1).astype(BF16)
    sbm = jnp.repeat(prm["sgu_b"][l].T, SGU_GROUP_DIM, axis=1).astype(F32)
    invf, invr = _pool_inverse_counts()
    mixer = [invf, invr, _block_diag(prm["pool_w"][l]).astype(BF16), row(prm["pool_scale"][l]),
             swc, sbm, _attention_bias(prm["rel_bias"][l]),
             prm["w_branch_a"][l].astype(BF16), prm["w_branch_b"][l].astype(BF16),
             prm["w_branch_c"][l].astype(BF16), prm["w_out"][l].astype(BF16)]

    wr = jnp.concatenate([prm["w_group_router"][l].T, jnp.zeros((SUBLANES - N_GROUPS, D_MODEL), F32),
                          prm["w_expert_router"][l].T], axis=0)
    wr_hi = wr.astype(BF16)
    wr_lo = (wr - wr_hi.astype(F32)).astype(BF16)
    br = jnp.concatenate([prm["b_group_router"][l], jnp.zeros((SUBLANES - N_GROUPS,), F32),
                          prm["b_expert_router"][l]]).reshape(-1, 1).astype(F32)
    router = dict(g=row(prm["ffn_norm_g"][l]), wr=jnp.concatenate([wr_hi, wr_lo], axis=0), br=br)

    wi = prm["w_expert_in"][l].reshape(N_GROUPS, EXPERTS_PER_GROUP, D_MODEL, 2, D_EXPERT)
    wi = jnp.transpose(wi, (0, 2, 3, 1, 4)).reshape(N_GROUPS, D_MODEL, 2 * GROUP_FF).astype(BF16)
    wo = prm["w_expert_out"][l].reshape(N_GROUPS, GROUP_FF, D_MODEL).astype(BF16)
    moe = dict(win=wi, wout=wo)

    ple = dict(g=row(prm["ple_norm_g"][l]), wpi=prm["w_ple_in"][l].astype(BF16),
               wg=prm["w_ple_gate"][l].astype(BF16))
    return inproj, mixer, router, moe, ple


def _sort_by_group(gid, chunk):
    gid = gid.reshape(-1, chunk)
    sidx = jnp.argsort(gid, axis=1, stable=True).astype(I32)
    cnts = jnp.sum(gid[:, :, None] == jnp.arange(N_GROUPS, dtype=I32)[None, None, :], axis=1).astype(I32)
    offs = jnp.cumsum(cnts, axis=1) - cnts
    return sidx.reshape(-1), offs.reshape(-1).astype(I32), cnts.reshape(-1)


def kernel(x, p, mix_norm_g, w_in, pool_w, pool_scale, sgu_norm_g, sgu_w, sgu_b, q_norm_g, k_norm_g, rel_bias, gate_b, w_branch_a, w_branch_b, w_branch_c, w_out, ffn_norm_g, w_group_router, b_group_router, w_expert_router, b_expert_router, w_expert_in, w_expert_out, ple_norm_g, w_ple_in, w_ple_gate):
    prm = dict(mix_norm_g=mix_norm_g, w_in=w_in, pool_w=pool_w, pool_scale=pool_scale, sgu_norm_g=sgu_norm_g,
               sgu_w=sgu_w, sgu_b=sgu_b, q_norm_g=q_norm_g, k_norm_g=k_norm_g, rel_bias=rel_bias, gate_b=gate_b,
               w_branch_a=w_branch_a, w_branch_b=w_branch_b, w_branch_c=w_branch_c, w_out=w_out,
               ffn_norm_g=ffn_norm_g, w_group_router=w_group_router, b_group_router=b_group_router,
               w_expert_router=w_expert_router, b_expert_router=b_expert_router, w_expert_in=w_expert_in,
               w_expert_out=w_expert_out, ple_norm_g=ple_norm_g, w_ple_in=w_ple_in, w_ple_gate=w_ple_gate)
    b, s, d = x.shape
    t = b * s
    depth = p.shape[0]
    chunk = min(MOE_CHUNK, t)
    assert d == D_MODEL and s % ROWS_MIX == 0 and t % ROWS_PROJ == 0 and t % chunk == 0
    assert s // ROWS_MIX >= HALO_BLOCKS + 1
    xf = x.reshape(t, d)
    for l in range(depth):
        c_in, c_mix, c_rt, c_moe, c_ple = _layer_consts(l, prm)
        a, u, v, q, k, vv, gates = _in_proj(xf, **c_in)
        xf = _mixer(xf, a, u, v, q, k, vv, gates, c_mix, s)
        hp, gid = _router(xf, **c_rt)
        sidx, offs, cnts = _sort_by_group(gid, chunk)
        mo = _moe(hp, sidx, offs, cnts, c_moe["win"], c_moe["wout"], chunk)
        xf = _ple(xf, mo, p[l].reshape(t, PLE_DIM), **c_ple)
    return xf.reshape(b, s, d)
```

```python
import functools

import jax
import jax.numpy as jnp
import numpy as np
from jax import lax
from jax.experimental import pallas as pl
from jax.experimental.pallas import tpu as pltpu

F32 = jnp.float32
BF16 = jnp.bfloat16
U32 = jnp.uint32
I32 = jnp.int32

D_MODEL = 1024
CHUNK = 64
PLE_DIM = 256
EPS = 1e-6
POOL_WINDOWS = (2, 4, 8, 16)
POOL_WIDTH = 256
POOL_GROUP_DIM = 64
SGU_BLOCK = 128
SGU_WIDTH = 256
SGU_GROUPS = 4
SGU_GROUP_DIM = 64
HEAD_DIM = 64
ATT_WIDTH = 512
ATT_HEADS = 8
LEFT_CHUNKS = 8
REL_CLIP = 128
N_BRANCH = 3
N_GROUPS = 4
EXPERTS_PER_GROUP = 8
D_EXPERT = 128
GROUP_FF = EXPERTS_PER_GROUP * D_EXPERT

C_A, C_U, C_V, C_Q, C_K, C_VV, C_G = 0, 256, 512, 768, 1280, 1792, 2304

LANES = 128
SUBLANES = 8
VMEM_LIMIT = 56 * 1024 * 1024

ROWS_INPROJ = 512
ROWS_PROJ = 512
ROWS_MIX = 256
HALO_BLOCKS = 2
POOL_HALO = 16
MOE_CHUNK = 2048
MOE_TILE = 256
MOE_STRIDE = MOE_TILE + 1
MOE_UNROLL = 32
PACK_W = D_MODEL // 2
PACK_ROWS = PACK_W // LANES
NEG = -1e30


def _const_spec(shape):
    nd = len(shape)
    return pl.BlockSpec(shape, lambda *_: (0,) * nd, pipeline_mode=pl.Buffered(1))


def _layer_spec(arr, l):
    nd = arr.ndim - 1
    return pl.BlockSpec((None,) + arr.shape[1:], lambda *_: (l,) + (0,) * nd, pipeline_mode=pl.Buffered(1))


def _params(n_axes):
    return pltpu.CompilerParams(dimension_semantics=("arbitrary",) * n_axes,
                                vmem_limit_bytes=VMEM_LIMIT)


def _rms(x, g):
    ms = jnp.mean(x * x, axis=-1, keepdims=True)
    return x * lax.rsqrt(ms + EPS) * g


def _gelu_tanh(x):
    c = np.float32(np.sqrt(2.0 / np.pi))
    return 0.5 * x * (1.0 + jnp.tanh(c * (x + np.float32(0.044715) * (x * x * x))))


def _sigmoid(x):
    return 0.5 * jnp.tanh(0.5 * x) + 0.5


def _dot(a, b):
    return jnp.dot(a, b, preferred_element_type=F32)


def _dot_nt(a, b):
    return lax.dot_general(a, b, (((1,), (1,)), ((), ())), preferred_element_type=F32)


def _pack_pairs(x):
    w = x.shape[1] // 2
    bits = pltpu.bitcast(x.astype(BF16).astype(F32), U32)
    return (bits[:, :w] >> 16) | (bits[:, w:] & np.uint32(0xFFFF0000))


def _unpack_pairs(p):
    lo = pltpu.bitcast(p << 16, F32)
    hi = pltpu.bitcast(p & np.uint32(0xFFFF0000), F32)
    return jnp.concatenate([lo, hi], axis=1)


def _slab_rows(j, n):
    return pl.ds(j, n, stride=SUBLANES)


def _inproj_kernel(x_ref, g_ref, w_ref, sgn_ref, gq_ref, gk_ref, hm_ref, gb_ref,
                   a_ref, u_ref, v_ref, q_ref, k_ref, vv_ref, gates_ref):
    h = _rms(x_ref[...], g_ref[...]).astype(BF16)

    def seg(c0, c1):
        return _dot(h, w_ref[:, c0:c1])

    def head_norm(z, g):
        ms = _dot((z * z).astype(BF16), hm_ref[...])
        return z * lax.rsqrt(ms + EPS) * g

    a_ref[...] = seg(C_A, C_U).astype(BF16)
    u_ref[...] = _gelu_tanh(seg(C_U, C_V)).astype(BF16)
    v_ref[...] = _rms(_gelu_tanh(seg(C_V, C_Q)), sgn_ref[...]).astype(BF16)
    q_ref[...] = (head_norm(seg(C_Q, C_K), gq_ref[...]) * np.float32(HEAD_DIM ** -0.5)).astype(BF16)
    k_ref[...] = head_norm(seg(C_K, C_VV), gk_ref[...]).astype(BF16)
    vv_ref[...] = seg(C_VV, C_G).astype(BF16)
    step = 512
    for c0 in range(0, N_BRANCH * D_MODEL, step):
        z = seg(C_G + c0, C_G + c0 + step) + gb_ref[:, c0:c0 + step]
        gates_ref[:, c0:c0 + step] = _sigmoid(z).astype(BF16)


def _in_proj(l, x2d, c):
    t = x2d.shape[0]
    tm = ROWS_INPROJ
    row = lambda w: pl.BlockSpec((tm, w), lambda i: (i, 0))
    widths = (POOL_WIDTH, SGU_WIDTH, SGU_WIDTH, ATT_WIDTH, ATT_WIDTH, ATT_WIDTH, N_BRANCH * D_MODEL)
    layer = [c["g"], c["w_in"], c["sgn"], c["gq"], c["gk"]]
    return pl.pallas_call(
        _inproj_kernel,
        grid=(t // tm,),
        in_specs=[row(D_MODEL)] + [_layer_spec(w, l) for w in layer]
                 + [_const_spec(c["hm"].shape), _layer_spec(c["gb"], l)],
        out_specs=[row(w) for w in widths],
        out_shape=[jax.ShapeDtypeStruct((t, w), BF16) for w in widths],
        compiler_params=_params(1),
        name="in_proj",
    )(x2d, *layer, c["hm"], c["gb"])


def _mixer_kernel(blocks_per_seq,
                  x_ref, a_ref, ap_ref, u_ref, v_ref, q_ref,
                  k0_ref, k1_ref, k2_ref, v0_ref, v1_ref, v2_ref, gates_ref,
                  invf_ref, invr_ref, bd_ref, ps_ref, swc_ref, sb_ref, bias_ref,
                  wa_ref, wb_ref, wc_ref, wo_ref,
                  o_ref, pool_buf):
    r = ROWS_MIX
    bi = pl.program_id(0) % blocks_per_seq
    first = bi == 0

    a = a_ref[...].astype(F32)
    halo = jnp.where(first, 0.0, ap_ref[r - POOL_HALO:, :].astype(F32))
    pool_buf[0:POOL_HALO, :] = jnp.zeros((POOL_HALO, POOL_WIDTH), F32)
    pool_buf[POOL_HALO:2 * POOL_HALO, :] = halo
    pool_buf[2 * POOL_HALO:, :] = a
    lane_a = lax.broadcasted_iota(I32, (r, POOL_WIDTH), 1) // POOL_GROUP_DIM
    wsum = None
    for gi, k in enumerate((1, 2, 4, 8)):
        cur = pool_buf[POOL_HALO:, :] + pool_buf[POOL_HALO - k:r + 2 * POOL_HALO - k, :]
        pool_buf[POOL_HALO:, :] = cur
        level = cur[POOL_HALO:, :]
        wsum = level if wsum is None else jnp.where(lane_a >= gi, level, wsum)
    inv = jnp.where(first, invf_ref[...], invr_ref[...])
    pooled = (wsum * inv - a).astype(BF16)
    ya = (_dot(pooled, bd_ref[...]) * ps_ref[...]).astype(BF16)

    vb = v_ref[...]
    lane_b = lax.broadcasted_iota(I32, (SGU_BLOCK, SGU_WIDTH), 1) // SGU_GROUP_DIM
    zero_b = jnp.zeros((SGU_BLOCK, SGU_WIDTH), BF16)
    mixed = []
    for s in range(r // SGU_BLOCK):
        blk = vb[s * SGU_BLOCK:(s + 1) * SGU_BLOCK, :]
        rhs = jnp.concatenate([jnp.where(lane_b == g, blk, zero_b) for g in range(SGU_GROUPS)], axis=0)
        mixed.append(_dot(swc_ref[...], rhs) + sb_ref[...])
    yb = (u_ref[...].astype(F32) * jnp.concatenate(mixed, axis=0)).astype(BF16)

    lane_c = lax.broadcasted_iota(I32, (r, LANES), 1)
    zero_q = jnp.zeros((r, LANES), BF16)
    k_refs = (k0_ref, k1_ref, k2_ref)
    v_refs = (v0_ref, v1_ref, v2_ref)
    blk_ok = (bi >= 2, bi >= 1, None)
    yc = []
    for pr in range(ATT_HEADS // 2):
        cols = slice(pr * LANES, (pr + 1) * LANES)
        qp = q_ref[:, cols]
        kps = [kr[:, cols] for kr in k_refs]
        vps = [vr[:, cols] for vr in v_refs]
        outs = []
        for s in range(2):
            hd = 2 * pr + s
            keep = (lane_c < HEAD_DIM) if s == 0 else (lane_c >= HEAD_DIM)
            qm = jnp.where(keep, qp, zero_q)
            parts = []
            for j in range(HALO_BLOCKS + 1):
                bias = bias_ref[hd, :, j * r:(j + 1) * r]
                if blk_ok[j] is not None:
                    bias = jnp.where(blk_ok[j], bias, NEG)
                parts.append(_dot_nt(qm, kps[j]) + bias)
            sc = jnp.concatenate(parts, axis=1)
            m = jnp.max(sc, axis=-1, keepdims=True)
            p = jnp.exp(sc - m)
            denom = jnp.sum(p, axis=-1, keepdims=True)
            pb = p.astype(BF16)
            o = _dot(pb[:, 0:r], vps[0])
            for j in range(1, HALO_BLOCKS + 1):
                o = o + _dot(pb[:, j * r:(j + 1) * r], vps[j])
            outs.append(o / denom)
        yc.append(jnp.where(lane_c < HEAD_DIM, outs[0], outs[1]))
    yc = jnp.concatenate(yc, axis=1).astype(BF16)

    d = D_MODEL
    merged = (gates_ref[:, 0:d].astype(F32) * _dot(ya, wa_ref[...])
              + gates_ref[:, d:2 * d].astype(F32) * _dot(yb, wb_ref[...])
              + gates_ref[:, 2 * d:3 * d].astype(F32) * _dot(yc, wc_ref[...]))
    o_ref[...] = x_ref[...] + _dot(merged.astype(BF16), wo_ref[...])


def _mixer(l, x2d, a, u, v, q, k, vv, gates, c, seq):
    t = x2d.shape[0]
    r = ROWS_MIX
    row = lambda w: pl.BlockSpec((r, w), lambda i: (i, 0))
    prev = lambda w, n: pl.BlockSpec((r, w), lambda i: (jnp.maximum(i - n, 0), 0))
    layer = [c["bd"], c["ps"], c["swc"], c["sbm"], c["bias"], c["wa"], c["wb"], c["wc"], c["wo"]]
    in_specs = [row(D_MODEL), row(POOL_WIDTH), prev(POOL_WIDTH, 1), row(SGU_WIDTH), row(SGU_WIDTH),
                row(ATT_WIDTH),
                prev(ATT_WIDTH, 2), prev(ATT_WIDTH, 1), row(ATT_WIDTH),
                prev(ATT_WIDTH, 2), prev(ATT_WIDTH, 1), row(ATT_WIDTH),
                row(N_BRANCH * D_MODEL),
                _const_spec(c["invf"].shape), _const_spec(c["invr"].shape)] + [_layer_spec(w, l) for w in layer]
    return pl.pallas_call(
        functools.partial(_mixer_kernel, seq // r),
        grid=(t // r,),
        in_specs=in_specs,
        out_specs=row(D_MODEL),
        out_shape=jax.ShapeDtypeStruct((t, D_MODEL), F32),
        scratch_shapes=[pltpu.VMEM((r + 2 * POOL_HALO, POOL_WIDTH), F32)],
        compiler_params=_params(1),
        name="mixer",
    )(x2d, a, a, u, v, q, k, k, k, vv, vv, vv, gates, c["invf"], c["invr"], *layer)


ROUTER_ROWS = SUBLANES + N_GROUPS * EXPERTS_PER_GROUP


def _router_kernel(x_ref, g_ref, wr_ref, br_ref, hp_ref, gid_ref):
    tm = x_ref.shape[0]
    h = _rms(x_ref[...], g_ref[...])
    h_hi = h.astype(BF16)
    h_lo = (h - h_hi.astype(F32)).astype(BF16)
    nr = ROUTER_ROWS
    l1 = _dot_nt(wr_ref[...], h_hi)
    l2 = _dot_nt(wr_ref[0:nr, :], h_lo)
    logits = l1[0:nr] + l1[nr:2 * nr] + l2 + br_ref[...]

    rows = lax.broadcasted_iota(I32, (SUBLANES, tm), 0).astype(F32)
    big = np.float32(1 << 20)
    gl = jnp.where(rows < N_GROUPS, logits[0:SUBLANES], NEG)
    gmax = jnp.max(gl, axis=0, keepdims=True)
    g_p = 1.0 / jnp.sum(jnp.exp(gl - gmax), axis=0, keepdims=True)
    g_idx = jnp.min(jnp.where(gl == gmax, rows, big), axis=0, keepdims=True)

    el = logits[SUBLANES:2 * SUBLANES]
    for gi in range(1, N_GROUPS):
        el = jnp.where(g_idx == gi, logits[SUBLANES * (gi + 1):SUBLANES * (gi + 2)], el)
    emax = jnp.max(el, axis=0, keepdims=True)
    esum = jnp.sum(jnp.exp(el - emax), axis=0, keepdims=True)
    i0 = jnp.min(jnp.where(el == emax, rows, big), axis=0, keepdims=True)
    el2 = jnp.where(rows == i0, -jnp.inf, el)
    m2 = jnp.max(el2, axis=0, keepdims=True)
    i1 = jnp.min(jnp.where(el2 == m2, rows, big), axis=0, keepdims=True)
    p0 = 1.0 / esum
    p1 = jnp.exp(m2 - emax) / esum
    w0 = g_p * p0 / (p0 + p1)
    w1 = g_p * p1 / (p0 + p1)
    gates = jnp.where(rows == i0, w0, 0.0) + jnp.where(rows == i1, w1, 0.0)

    gates_t = jnp.concatenate([gates, jnp.zeros((LANES - SUBLANES, tm), F32)], axis=0).T
    packed = _pack_pairs(h)
    for j in range(PACK_ROWS):
        hp_ref[_slab_rows(j, tm), :] = packed[:, j * LANES:(j + 1) * LANES]
    hp_ref[_slab_rows(PACK_ROWS, tm), :] = pltpu.bitcast(gates_t, U32)
    for j in range(PACK_ROWS + 1, SUBLANES):
        hp_ref[_slab_rows(j, tm), :] = jnp.zeros((tm, LANES), U32)
    gid_ref[...] = g_idx.astype(I32).reshape(1, 1, tm)


def _router(l, x2d, c):
    t = x2d.shape[0]
    tm = ROWS_PROJ
    layer = [c["g"], c["wr"], c["br"]]
    return pl.pallas_call(
        _router_kernel,
        grid=(t // tm,),
        in_specs=[pl.BlockSpec((tm, D_MODEL), lambda i: (i, 0))] + [_layer_spec(w, l) for w in layer],
        out_specs=[pl.BlockSpec((tm * SUBLANES, LANES), lambda i: (i, 0)),
                   pl.BlockSpec((1, 1, tm), lambda i: (i, 0, 0))],
        out_shape=[jax.ShapeDtypeStruct((t * SUBLANES, LANES), U32),
                   jax.ShapeDtypeStruct((t // tm, 1, tm), I32)],
        compiler_params=_params(1),
        name="router",
    )(x2d, *layer)


def _moe_kernel(chunk, sidx_ref, offs_ref, cnts_ref, hp_ref, win_ref, wout_ref, o_ref, xt_ref, yt_ref):
    c = pl.program_id(0)
    g = pl.program_id(1)
    off = offs_ref[c * N_GROUPS + g]
    n = cnts_ref[c * N_GROUPS + g]
    base = c * chunk
    last = off + n - 1
    tr = MOE_TILE
    st = MOE_STRIDE

    def tile_body(t, carry):
        r0 = off + t * tr

        def slab(rr):
            start = pl.multiple_of(sidx_ref[base + jnp.minimum(r0 + rr, last)], SUBLANES)
            return pl.ds(start, SUBLANES)

        def gather(i, cr):
            for uu in range(MOE_UNROLL):
                rr = i * MOE_UNROLL + uu
                xt_ref[pl.ds(rr, SUBLANES, stride=st), :] = hp_ref[slab(rr), :]
            return cr

        lax.fori_loop(0, tr // MOE_UNROLL, gather, 0)

        packed = jnp.concatenate([xt_ref[pl.ds(j * st, tr), :] for j in range(PACK_ROWS)], axis=1)
        xb = _unpack_pairs(packed).astype(BF16)
        gates = pltpu.bitcast(xt_ref[pl.ds(PACK_ROWS * st, tr), :], F32)
        acts = []
        for e in range(EXPERTS_PER_GROUP):
            hu = _dot(xb, win_ref[e])
            hg = hu[:, 0:D_EXPERT]
            acts.append((hg * _sigmoid(hg) * hu[:, D_EXPERT:] * gates[:, e:e + 1]).astype(BF16))
        y = _dot(jnp.concatenate(acts, axis=1), wout_ref[...])
        for j in range(SUBLANES):
            yt_ref[pl.ds(j * st, tr), :] = y[:, j * LANES:(j + 1) * LANES]

        def scatter(i, cr):
            for uu in range(MOE_UNROLL):
                rr = i * MOE_UNROLL + uu
                o_ref[slab(rr), :] = yt_ref[pl.ds(rr, SUBLANES, stride=st), :]
            return cr

        lax.fori_loop(0, tr // MOE_UNROLL, scatter, 0)
        return carry

    lax.fori_loop(0, (n + tr - 1) // tr, tile_body, 0)


def _moe(l, hp, sidx, offs, cnts, c, chunk):
    t = hp.shape[0] // SUBLANES
    win, wout = c["win"], c["wout"]
    grid_spec = pltpu.PrefetchScalarGridSpec(
        num_scalar_prefetch=3,
        grid=(t // chunk, N_GROUPS),
        in_specs=[pl.BlockSpec((chunk * SUBLANES, LANES), lambda ci, g, *_: (ci, 0)),
                  pl.BlockSpec((None, EXPERTS_PER_GROUP) + win.shape[2:], lambda ci, g, *_: (l, g, 0, 0)),
                  pl.BlockSpec((None, None) + wout.shape[2:], lambda ci, g, *_: (l, g, 0, 0))],
        out_specs=pl.BlockSpec((chunk * SUBLANES, LANES), lambda ci, g, *_: (ci, 0)),
        scratch_shapes=[pltpu.VMEM((MOE_STRIDE * SUBLANES, LANES), U32),
                        pltpu.VMEM((MOE_STRIDE * SUBLANES, LANES), F32)],
    )
    return pl.pallas_call(
        functools.partial(_moe_kernel, chunk),
        grid_spec=grid_spec,
        out_shape=jax.ShapeDtypeStruct((t * SUBLANES, LANES), F32),
        compiler_params=_params(2),
        name="moe",
    )(sidx, offs, cnts, hp, win, wout)


def _ple_kernel(x_ref, mo_ref, p_ref, g_ref, wpi_ref, wg_ref, o_ref):
    tm = x_ref.shape[0]
    mo = jnp.concatenate([mo_ref[_slab_rows(j, tm), :] for j in range(SUBLANES)], axis=1)
    x = x_ref[...] + mo
    hn = _rms(x, g_ref[...]).astype(BF16)
    gate = _sigmoid(_dot(hn, wg_ref[...]))
    emb = _dot(p_ref[...].astype(BF16), wpi_ref[...])
    o_ref[...] = x + emb * gate


def _ple(l, x2d, mo, p3d, c):
    t = x2d.shape[0]
    tm = ROWS_PROJ
    row = lambda w: pl.BlockSpec((tm, w), lambda i: (i, 0))
    layer = [c["g"], c["wpi"], c["wg"]]
    return pl.pallas_call(
        _ple_kernel,
        grid=(t // tm,),
        in_specs=[row(D_MODEL), pl.BlockSpec((tm * SUBLANES, LANES), lambda i: (i, 0)),
                  pl.BlockSpec((None, tm, PLE_DIM), lambda i: (l, i, 0))] + [_layer_spec(w, l) for w in layer],
        out_specs=row(D_MODEL),
        out_shape=jax.ShapeDtypeStruct((t, D_MODEL), F32),
        compiler_params=_params(1),
        name="ple",
    )(x2d, mo, p3d, *layer)


def _attention_bias(rel_bias):
    r = ROWS_MIX
    nk = (HALO_BLOCKS + 1) * r
    qi = np.arange(r)[:, None]
    kj = np.arange(nk)[None, :]
    qc = qi // CHUNK
    kc = kj // CHUNK
    valid = (kc >= qc) & (kc <= qc + LEFT_CHUNKS)
    p = nk + r
    off = np.arange(p)
    off = np.where(off < nk, off, off - p)
    rel = np.clip(HALO_BLOCKS * r - off, -REL_CLIP, REL_CLIP) + REL_CLIP
    per_off = rel_bias[..., rel].astype(F32)
    lead = per_off.shape[:-1]
    band = jnp.tile(per_off, (1, 1, r))[..., :r * (p - 1)].reshape(lead + (r, p - 1))[..., :nk]
    return jnp.where(jnp.asarray(valid), band, NEG)


def _pool_inverse_counts():
    r = ROWS_MIX
    win = np.repeat(np.asarray(POOL_WINDOWS, np.float32), POOL_GROUP_DIM)[None, :]
    pos = np.arange(1, r + 1, dtype=np.float32)[:, None]
    return jnp.asarray(1.0 / np.minimum(pos, win), F32), jnp.asarray(1.0 / win, F32)


def _block_diag(w):
    nl, g, c, d = w.shape
    eye = jnp.eye(g, dtype=w.dtype)
    return jnp.einsum("gh,lgcd->lgchd", eye, w).reshape(nl, g * c, g * d)


def _prepare(prm):
    nl = prm["w_in"].shape[0]
    row = lambda v: v.reshape(nl, 1, -1).astype(F32)
    hm = _block_diag(jnp.full((1, ATT_HEADS, HEAD_DIM, HEAD_DIM), 1.0 / HEAD_DIM, F32))[0].astype(BF16)
    inproj = dict(
        g=row(prm["mix_norm_g"]), w_in=prm["w_in"].astype(BF16), sgn=row(prm["sgu_norm_g"]),
        gq=row(jnp.tile(prm["q_norm_g"], (1, ATT_HEADS))), gk=row(jnp.tile(prm["k_norm_g"], (1, ATT_HEADS))),
        hm=hm, gb=row(prm["gate_b"]))

    i = np.arange(SGU_BLOCK)
    tri = (i[None, :] // CHUNK) <= (i[:, None] // CHUNK)
    sw = jnp.where(jnp.asarray(tri), prm["sgu_w"], 0.0)
    swc = jnp.concatenate([sw[:, g] for g in range(SGU_GROUPS)], axis=2).astype(BF16)
    sbm = jnp.repeat(jnp.swapaxes(prm["sgu_b"], 1, 2), SGU_GROUP_DIM, axis=2).astype(F32)
    invf, invr = _pool_inverse_counts()
    mixer = dict(invf=invf, invr=invr, bd=_block_diag(prm["pool_w"]).astype(BF16), ps=row(prm["pool_scale"]),
                 swc=swc, sbm=sbm, bias=_attention_bias(prm["rel_bias"]),
                 wa=prm["w_branch_a"].astype(BF16), wb=prm["w_branch_b"].astype(BF16),
                 wc=prm["w_branch_c"].astype(BF16), wo=prm["w_out"].astype(BF16))

    pad = jnp.zeros((nl, SUBLANES - N_GROUPS, D_MODEL), F32)
    wr = jnp.concatenate([jnp.swapaxes(prm["w_group_router"], 1, 2), pad,
                          jnp.swapaxes(prm["w_expert_router"], 1, 2)], axis=1)
    wr_hi = wr.astype(BF16)
    wr_lo = (wr - wr_hi.astype(F32)).astype(BF16)
    br = jnp.concatenate([prm["b_group_router"], jnp.zeros((nl, SUBLANES - N_GROUPS), F32),
                          prm["b_expert_router"]], axis=1).reshape(nl, -1, 1).astype(F32)
    router = dict(g=row(prm["ffn_norm_g"]), wr=jnp.concatenate([wr_hi, wr_lo], axis=1), br=br)

    moe = dict(win=prm["w_expert_in"].astype(BF16),
               wout=prm["w_expert_out"].astype(BF16).reshape(nl, N_GROUPS, GROUP_FF, D_MODEL))

    ple = dict(g=row(prm["ple_norm_g"]), wpi=prm["w_ple_in"].astype(BF16), wg=prm["w_ple_gate"].astype(BF16))
    return inproj, mixer, router, moe, ple


def _sort_by_group(gid, chunk):
    gid = gid.reshape(-1, chunk)
    sidx = jnp.argsort(gid, axis=1, stable=True).astype(I32) * SUBLANES
    cnts = jnp.sum(gid[:, :, None] == jnp.arange(N_GROUPS, dtype=I32)[None, None, :], axis=1).astype(I32)
    offs = jnp.cumsum(cnts, axis=1) - cnts
    return sidx.reshape(-1), offs.reshape(-1).astype(I32), cnts.reshape(-1)


def kernel(x, p, mix_norm_g, w_in, pool_w, pool_scale, sgu_norm_g, sgu_w, sgu_b, q_norm_g, k_norm_g, rel_bias, gate_b, w_branch_a, w_branch_b, w_branch_c, w_out, ffn_norm_g, w_group_router, b_group_router, w_expert_router, b_expert_router, w_expert_in, w_expert_out, ple_norm_g, w_ple_in, w_ple_gate):
    prm = dict(mix_norm_g=mix_norm_g, w_in=w_in, pool_w=pool_w, pool_scale=pool_scale, sgu_norm_g=sgu_norm_g,
               sgu_w=sgu_w, sgu_b=sgu_b, q_norm_g=q_norm_g, k_norm_g=k_norm_g, rel_bias=rel_bias, gate_b=gate_b,
               w_branch_a=w_branch_a, w_branch_b=w_branch_b, w_branch_c=w_branch_c, w_out=w_out,
               ffn_norm_g=ffn_norm_g, w_group_router=w_group_router, b_group_router=b_group_router,
               w_expert_router=w_expert_router, b_expert_router=b_expert_router, w_expert_in=w_expert_in,
               w_expert_out=w_expert_out, ple_norm_g=ple_norm_g, w_ple_in=w_ple_in, w_ple_gate=w_ple_gate)
    b, s, d = x.shape
    t = b * s
    depth = p.shape[0]
    chunk = min(MOE_CHUNK, t)
    assert d == D_MODEL and s % ROWS_MIX == 0 and t % ROWS_PROJ == 0 and t % ROWS_INPROJ == 0 and t % chunk == 0
    assert s // ROWS_MIX >= HALO_BLOCKS + 1
    c_in, c_mix, c_rt, c_moe, c_ple = _prepare(prm)
    p3d = p.reshape(depth, t, PLE_DIM)
    xf = x.reshape(t, d)
    for l in range(depth):
        a, u, v, q, k, vv, gates = _in_proj(l, xf, c_in)
        xf = _mixer(l, xf, a, u, v, q, k, vv, gates, c_mix, s)
        hp, gid = _router(l, xf, c_rt)
        sidx, offs, cnts = _sort_by_group(gid, chunk)
        mo = _moe(l, hp, sidx, offs, cnts, c_moe, chunk)
        xf = _ple(l, xf, mo, p3d, c_ple)
    return xf.reshape(b, s, d)
```

```python
import functools

import jax
import jax.numpy as jnp
import numpy as np
from jax import lax
from jax.experimental import pallas as pl
from jax.experimental.pallas import tpu as pltpu

F32 = jnp.float32
BF16 = jnp.bfloat16
U32 = jnp.uint32
I32 = jnp.int32

D_MODEL = 1024
CHUNK = 64
PLE_DIM = 256
EPS = 1e-6
POOL_WINDOWS = (2, 4, 8, 16)
POOL_WIDTH = 256
POOL_GROUP_DIM = 64
SGU_BLOCK = 128
SGU_WIDTH = 256
SGU_GROUPS = 4
SGU_GROUP_DIM = 64
HEAD_DIM = 64
ATT_WIDTH = 512
ATT_HEADS = 8
LEFT_CHUNKS = 8
REL_CLIP = 128
N_BRANCH = 3
N_GROUPS = 4
EXPERTS_PER_GROUP = 8
D_EXPERT = 128
GROUP_FF = EXPERTS_PER_GROUP * D_EXPERT

C_A, C_U, C_V, C_Q, C_K, C_VV, C_G = 0, 256, 512, 768, 1280, 1792, 2304

LANES = 128
SUBLANES = 8
VMEM_LIMIT = 56 * 1024 * 1024

ROWS_INPROJ = 512
ROWS_PROJ = 512
ROWS_MIX = 256
HALO_BLOCKS = 2
POOL_HALO = 16
MOE_CHUNK = 2048
MOE_TILE = 256
MOE_STRIDE = MOE_TILE + 1
MOE_UNROLL = 32
PACK_W = D_MODEL // 2
PACK_ROWS = PACK_W // LANES
NEG = -1e30
LOG2E = float(np.log2(np.e))


def _const_spec(shape):
    nd = len(shape)
    return pl.BlockSpec(shape, lambda *_: (0,) * nd, pipeline_mode=pl.Buffered(1))


def _layer_spec(arr, l):
    nd = arr.ndim - 1
    return pl.BlockSpec((None,) + arr.shape[1:], lambda *_: (l,) + (0,) * nd, pipeline_mode=pl.Buffered(1))


def _params(n_axes):
    return pltpu.CompilerParams(dimension_semantics=("arbitrary",) * n_axes,
                                vmem_limit_bytes=VMEM_LIMIT)


def _rms(x, g):
    ms = jnp.mean(x * x, axis=-1, keepdims=True)
    return x * lax.rsqrt(ms + EPS) * g


def _gelu_tanh(x):
    c = np.float32(np.sqrt(2.0 / np.pi))
    return 0.5 * x * (1.0 + jnp.tanh(c * (x + np.float32(0.044715) * (x * x * x))))


def _sigmoid(x):
    return 0.5 * jnp.tanh(0.5 * x) + 0.5


def _dot(a, b):
    return jnp.dot(a, b, preferred_element_type=F32)


def _dot_nt(a, b):
    return lax.dot_general(a, b, (((1,), (1,)), ((), ())), preferred_element_type=F32)


def _pack_pairs(x):
    w = x.shape[1] // 2
    bits = pltpu.bitcast(x.astype(BF16).astype(F32), U32)
    return (bits[:, :w] >> 16) | (bits[:, w:] & np.uint32(0xFFFF0000))


def _unpack_pairs(p):
    lo = pltpu.bitcast(p << 16, F32)
    hi = pltpu.bitcast(p & np.uint32(0xFFFF0000), F32)
    return jnp.concatenate([lo, hi], axis=1)


def _slab_rows(j, n):
    return pl.ds(j, n, stride=SUBLANES)


def _inproj_kernel(x_ref, g_ref, w_ref, sgn_ref, gq_ref, gk_ref, hm_ref, gb_ref,
                   a_ref, u_ref, v_ref, q_ref, k_ref, vv_ref, gates_ref):
    h = _rms(x_ref[...], g_ref[...]).astype(BF16)

    def seg(c0, c1):
        return _dot(h, w_ref[:, c0:c1])

    def head_norm(z, g):
        ms = _dot((z * z).astype(BF16), hm_ref[...])
        return z * lax.rsqrt(ms + EPS) * g

    a_ref[...] = seg(C_A, C_U).astype(BF16)
    u_ref[...] = _gelu_tanh(seg(C_U, C_V)).astype(BF16)
    v_ref[...] = _rms(_gelu_tanh(seg(C_V, C_Q)), sgn_ref[...]).astype(BF16)
    q_ref[...] = (head_norm(seg(C_Q, C_K), gq_ref[...]) * np.float32(HEAD_DIM ** -0.5 * LOG2E)).astype(BF16)
    k_ref[...] = head_norm(seg(C_K, C_VV), gk_ref[...]).astype(BF16)
    vv_ref[...] = seg(C_VV, C_G).astype(BF16)
    step = 512
    for c0 in range(0, N_BRANCH * D_MODEL, step):
        z = seg(C_G + c0, C_G + c0 + step) + gb_ref[:, c0:c0 + step]
        gates_ref[:, c0:c0 + step] = _sigmoid(z).astype(BF16)


def _in_proj(l, x2d, c):
    t = x2d.shape[0]
    tm = ROWS_INPROJ
    row = lambda w: pl.BlockSpec((tm, w), lambda i: (i, 0))
    widths = (POOL_WIDTH, SGU_WIDTH, SGU_WIDTH, ATT_WIDTH, ATT_WIDTH, ATT_WIDTH, N_BRANCH * D_MODEL)
    layer = [c["g"], c["w_in"], c["sgn"], c["gq"], c["gk"]]
    return pl.pallas_call(
        _inproj_kernel,
        grid=(t // tm,),
        in_specs=[row(D_MODEL)] + [_layer_spec(w, l) for w in layer]
                 + [_const_spec(c["hm"].shape), _layer_spec(c["gb"], l)],
        out_specs=[row(w) for w in widths],
        out_shape=[jax.ShapeDtypeStruct((t, w), BF16) for w in widths],
        compiler_params=_params(1),
        name="in_proj",
    )(x2d, *layer, c["hm"], c["gb"])


def _mixer_kernel(blocks_per_seq,
                  x_ref, a_ref, ap_ref, u_ref, v_ref, q_ref,
                  k0_ref, k1_ref, k2_ref, v0_ref, v1_ref, v2_ref, gates_ref,
                  invf_ref, invr_ref, bd_ref, ps_ref, swc_ref, sb_ref, bias_ref,
                  wa_ref, wb_ref, wc_ref, wo_ref, gf_ref, wr_ref, br_ref,
                  o_ref, hp_ref, gid_ref, pool_buf):
    r = ROWS_MIX
    bi = pl.program_id(0) % blocks_per_seq
    first = bi == 0

    a = a_ref[...].astype(F32)
    halo = jnp.where(first, 0.0, ap_ref[r - POOL_HALO:, :].astype(F32))
    pool_buf[0:POOL_HALO, :] = jnp.zeros((POOL_HALO, POOL_WIDTH), F32)
    pool_buf[POOL_HALO:2 * POOL_HALO, :] = halo
    pool_buf[2 * POOL_HALO:, :] = a
    lane_a = lax.broadcasted_iota(I32, (r, POOL_WIDTH), 1) // POOL_GROUP_DIM
    wsum = None
    for gi, k in enumerate((1, 2, 4, 8)):
        cur = pool_buf[POOL_HALO:, :] + pool_buf[POOL_HALO - k:r + 2 * POOL_HALO - k, :]
        pool_buf[POOL_HALO:, :] = cur
        level = cur[POOL_HALO:, :]
        wsum = level if wsum is None else jnp.where(lane_a >= gi, level, wsum)
    inv = jnp.where(first, invf_ref[...], invr_ref[...])
    pooled = (wsum * inv - a).astype(BF16)
    ya = (_dot(pooled, bd_ref[...]) * ps_ref[...]).astype(BF16)

    vb = v_ref[...]
    lane_b = lax.broadcasted_iota(I32, (SGU_BLOCK, SGU_WIDTH), 1) // SGU_GROUP_DIM
    zero_b = jnp.zeros((SGU_BLOCK, SGU_WIDTH), BF16)
    mixed = []
    for s in range(r // SGU_BLOCK):
        blk = vb[s * SGU_BLOCK:(s + 1) * SGU_BLOCK, :]
        rhs = jnp.concatenate([jnp.where(lane_b == g, blk, zero_b) for g in range(SGU_GROUPS)], axis=0)
        mixed.append(_dot(swc_ref[...], rhs) + sb_ref[...])
    yb = (u_ref[...].astype(F32) * jnp.concatenate(mixed, axis=0)).astype(BF16)

    lane_c = lax.broadcasted_iota(I32, (r, LANES), 1)
    zero_q = jnp.zeros((r, LANES), BF16)
    k_refs = (k0_ref, k1_ref, k2_ref)
    v_refs = (v0_ref, v1_ref, v2_ref)
    blk_ok = (bi >= 2, bi >= 1, None)
    yc = []
    for pr in range(ATT_HEADS // 2):
        cols = slice(pr * LANES, (pr + 1) * LANES)
        qp = q_ref[:, cols]
        kps = [kr[:, cols] for kr in k_refs]
        vps = [vr[:, cols] for vr in v_refs]
        outs = []
        for s in range(2):
            hd = 2 * pr + s
            keep = (lane_c < HEAD_DIM) if s == 0 else (lane_c >= HEAD_DIM)
            qm = jnp.where(keep, qp, zero_q)
            parts = []
            for j in range(HALO_BLOCKS + 1):
                bias = bias_ref[hd, :, j * r:(j + 1) * r]
                if blk_ok[j] is not None:
                    bias = jnp.where(blk_ok[j], bias, NEG)
                parts.append(_dot_nt(qm, kps[j]) + bias)
            sc = jnp.concatenate(parts, axis=1)
            m = jnp.max(sc, axis=-1, keepdims=True)
            p = jnp.exp2(sc - m)
            denom = jnp.sum(p, axis=-1, keepdims=True)
            pb = p.astype(BF16)
            o = _dot(pb[:, 0:r], vps[0])
            for j in range(1, HALO_BLOCKS + 1):
                o = o + _dot(pb[:, j * r:(j + 1) * r], vps[j])
            outs.append(o / denom)
        yc.append(jnp.where(lane_c < HEAD_DIM, outs[0], outs[1]))
    yc = jnp.concatenate(yc, axis=1).astype(BF16)

    d = D_MODEL
    merged = (gates_ref[:, 0:d].astype(F32) * _dot(ya, wa_ref[...])
              + gates_ref[:, d:2 * d].astype(F32) * _dot(yb, wb_ref[...])
              + gates_ref[:, 2 * d:3 * d].astype(F32) * _dot(yc, wc_ref[...]))
    x1 = x_ref[...] + _dot(merged.astype(BF16), wo_ref[...])
    o_ref[...] = x1

    _route(x1, gf_ref, wr_ref, br_ref, hp_ref, gid_ref)


def _mixer(l, x2d, a, u, v, q, k, vv, gates, c, seq):
    t = x2d.shape[0]
    r = ROWS_MIX
    row = lambda w: pl.BlockSpec((r, w), lambda i: (i, 0))
    prev = lambda w, n: pl.BlockSpec((r, w), lambda i: (jnp.maximum(i - n, 0), 0))
    layer = [c["bd"], c["ps"], c["swc"], c["sbm"], c["bias"], c["wa"], c["wb"], c["wc"], c["wo"],
             c["gf"], c["wr"], c["br"]]
    in_specs = [row(D_MODEL), row(POOL_WIDTH), prev(POOL_WIDTH, 1), row(SGU_WIDTH), row(SGU_WIDTH),
                row(ATT_WIDTH),
                prev(ATT_WIDTH, 2), prev(ATT_WIDTH, 1), row(ATT_WIDTH),
                prev(ATT_WIDTH, 2), prev(ATT_WIDTH, 1), row(ATT_WIDTH),
                row(N_BRANCH * D_MODEL),
                _const_spec(c["invf"].shape), _const_spec(c["invr"].shape)] + [_layer_spec(w, l) for w in layer]
    return pl.pallas_call(
        functools.partial(_mixer_kernel, seq // r),
        grid=(t // r,),
        in_specs=in_specs,
        out_specs=[row(D_MODEL), pl.BlockSpec((r * SUBLANES, LANES), lambda i: (i, 0)),
                   pl.BlockSpec((1, 1, r), lambda i: (i, 0, 0))],
        out_shape=[jax.ShapeDtypeStruct((t, D_MODEL), F32), jax.ShapeDtypeStruct((t * SUBLANES, LANES), U32),
                   jax.ShapeDtypeStruct((t // r, 1, r), I32)],
        scratch_shapes=[pltpu.VMEM((r + 2 * POOL_HALO, POOL_WIDTH), F32)],
        compiler_params=_params(1),
        name="mixer",
    )(x2d, a, a, u, v, q, k, k, k, vv, vv, vv, gates, c["invf"], c["invr"], *layer)


ROUTER_ROWS = SUBLANES + N_GROUPS * EXPERTS_PER_GROUP


def _route(x, g_ref, wr_ref, br_ref, hp_ref, gid_ref):
    tm = x.shape[0]
    h = _rms(x, g_ref[...])
    h_hi = h.astype(BF16)
    h_lo = (h - h_hi.astype(F32)).astype(BF16)
    nr = ROUTER_ROWS
    l1 = _dot_nt(wr_ref[...], h_hi)
    l2 = _dot_nt(wr_ref[0:nr, :], h_lo)
    logits = l1[0:nr] + l1[nr:2 * nr] + l2 + br_ref[...]

    rows = lax.broadcasted_iota(I32, (SUBLANES, tm), 0).astype(F32)
    big = np.float32(1 << 20)
    gl = jnp.where(rows < N_GROUPS, logits[0:SUBLANES], NEG)
    gmax = jnp.max(gl, axis=0, keepdims=True)
    g_p = 1.0 / jnp.sum(jnp.exp(gl - gmax), axis=0, keepdims=True)
    g_idx = jnp.min(jnp.where(gl == gmax, rows, big), axis=0, keepdims=True)

    el = logits[SUBLANES:2 * SUBLANES]
    for gi in range(1, N_GROUPS):
        el = jnp.where(g_idx == gi, logits[SUBLANES * (gi + 1):SUBLANES * (gi + 2)], el)
    emax = jnp.max(el, axis=0, keepdims=True)
    esum = jnp.sum(jnp.exp(el - emax), axis=0, keepdims=True)
    i0 = jnp.min(jnp.where(el == emax, rows, big), axis=0, keepdims=True)
    el2 = jnp.where(rows == i0, -jnp.inf, el)
    m2 = jnp.max(el2, axis=0, keepdims=True)
    i1 = jnp.min(jnp.where(el2 == m2, rows, big), axis=0, keepdims=True)
    p0 = 1.0 / esum
    p1 = jnp.exp(m2 - emax) / esum
    w0 = g_p * p0 / (p0 + p1)
    w1 = g_p * p1 / (p0 + p1)
    gates = jnp.where(rows == i0, w0, 0.0) + jnp.where(rows == i1, w1, 0.0)

    gates_t = jnp.concatenate([gates, jnp.zeros((LANES - SUBLANES, tm), F32)], axis=0).T
    packed = _pack_pairs(h)
    for j in range(PACK_ROWS):
        hp_ref[_slab_rows(j, tm), :] = packed[:, j * LANES:(j + 1) * LANES]
    hp_ref[_slab_rows(PACK_ROWS, tm), :] = pltpu.bitcast(gates_t, U32)
    for j in range(PACK_ROWS + 1, SUBLANES):
        hp_ref[_slab_rows(j, tm), :] = jnp.zeros((tm, LANES), U32)
    gid_ref[...] = g_idx.astype(I32).reshape(1, 1, tm)


def _moe_kernel(chunk, sidx_ref, offs_ref, cnts_ref, hp_ref, win_ref, wout_ref, o_ref, xt_ref, yt_ref):
    c = pl.program_id(0)
    g = pl.program_id(1)
    off = offs_ref[c * N_GROUPS + g]
    n = cnts_ref[c * N_GROUPS + g]
    base = c * chunk
    last = off + n - 1
    tr = MOE_TILE
    st = MOE_STRIDE

    def tile_body(t, carry):
        r0 = off + t * tr

        def slab(rr):
            start = pl.multiple_of(sidx_ref[base + jnp.minimum(r0 + rr, last)], SUBLANES)
            return pl.ds(start, SUBLANES)

        def gather(i, cr):
            for uu in range(MOE_UNROLL):
                rr = i * MOE_UNROLL + uu
                xt_ref[pl.ds(rr, SUBLANES, stride=st), :] = hp_ref[slab(rr), :]
            return cr

        lax.fori_loop(0, tr // MOE_UNROLL, gather, 0)

        packed = jnp.concatenate([xt_ref[pl.ds(j * st, tr), :] for j in range(PACK_ROWS)], axis=1)
        xb = _unpack_pairs(packed).astype(BF16)
        gates = pltpu.bitcast(xt_ref[pl.ds(PACK_ROWS * st, tr), :], F32)
        acts = []
        for e in range(EXPERTS_PER_GROUP):
            hu = _dot(xb, win_ref[e])
            hg = hu[:, 0:D_EXPERT]
            acts.append((hg * _sigmoid(hg) * hu[:, D_EXPERT:] * gates[:, e:e + 1]).astype(BF16))
        y = _dot(jnp.concatenate(acts, axis=1), wout_ref[...].reshape(GROUP_FF, D_MODEL))
        for j in range(SUBLANES):
            yt_ref[pl.ds(j * st, tr), :] = y[:, j * LANES:(j + 1) * LANES]

        def scatter(i, cr):
            for uu in range(MOE_UNROLL):
                rr = i * MOE_UNROLL + uu
                o_ref[slab(rr), :] = yt_ref[pl.ds(rr, SUBLANES, stride=st), :]
            return cr

        lax.fori_loop(0, tr // MOE_UNROLL, scatter, 0)
        return carry

    lax.fori_loop(0, (n + tr - 1) // tr, tile_body, 0)


def _moe(l, hp, sidx, offs, cnts, c, chunk):
    t = hp.shape[0] // SUBLANES
    win, wout = c["win"], c["wout"]
    grid_spec = pltpu.PrefetchScalarGridSpec(
        num_scalar_prefetch=3,
        grid=(t // chunk, N_GROUPS),
        in_specs=[pl.BlockSpec((chunk * SUBLANES, LANES), lambda ci, g, *_: (ci, 0)),
                  pl.BlockSpec((None, EXPERTS_PER_GROUP) + win.shape[2:], lambda ci, g, *_: (l, g, 0, 0)),
                  pl.BlockSpec((None, EXPERTS_PER_GROUP) + wout.shape[2:], lambda ci, g, *_: (l, g, 0, 0))],
        out_specs=pl.BlockSpec((chunk * SUBLANES, LANES), lambda ci, g, *_: (ci, 0)),
        scratch_shapes=[pltpu.VMEM((MOE_STRIDE * SUBLANES, LANES), U32),
                        pltpu.VMEM((MOE_STRIDE * SUBLANES, LANES), F32)],
    )
    return pl.pallas_call(
        functools.partial(_moe_kernel, chunk),
        grid_spec=grid_spec,
        out_shape=jax.ShapeDtypeStruct((t * SUBLANES, LANES), F32),
        compiler_params=_params(2),
        name="moe",
    )(sidx, offs, cnts, hp, win, wout)


def _ple_kernel(x_ref, mo_ref, p_ref, g_ref, wpi_ref, wg_ref, o_ref):
    tm = x_ref.shape[0]
    mo = jnp.concatenate([mo_ref[_slab_rows(j, tm), :] for j in range(SUBLANES)], axis=1)
    x = x_ref[...] + mo
    hn = _rms(x, g_ref[...]).astype(BF16)
    gate = _sigmoid(_dot(hn, wg_ref[...]))
    emb = _dot(p_ref[...].astype(BF16), wpi_ref[...])
    o_ref[...] = x + emb * gate


def _ple(l, x2d, mo, p3d, c):
    t = x2d.shape[0]
    tm = ROWS_PROJ
    row = lambda w: pl.BlockSpec((tm, w), lambda i: (i, 0))
    layer = [c["g"], c["wpi"], c["wg"]]
    return pl.pallas_call(
        _ple_kernel,
        grid=(t // tm,),
        in_specs=[row(D_MODEL), pl.BlockSpec((tm * SUBLANES, LANES), lambda i: (i, 0)),
                  pl.BlockSpec((None, tm, PLE_DIM), lambda i: (l, i, 0))] + [_layer_spec(w, l) for w in layer],
        out_specs=row(D_MODEL),
        out_shape=jax.ShapeDtypeStruct((t, D_MODEL), F32),
        compiler_params=_params(1),
        name="ple",
    )(x2d, mo, p3d, *layer)


def _attention_bias(rel_bias):
    r = ROWS_MIX
    nk = (HALO_BLOCKS + 1) * r
    qi = np.arange(r)[:, None]
    kj = np.arange(nk)[None, :]
    qc = qi // CHUNK
    kc = kj // CHUNK
    valid = (kc >= qc) & (kc <= qc + LEFT_CHUNKS)
    p = nk + r
    off = np.arange(p)
    off = np.where(off < nk, off, off - p)
    rel = np.clip(HALO_BLOCKS * r - off, -REL_CLIP, REL_CLIP) + REL_CLIP
    per_off = rel_bias[..., rel].astype(F32)
    lead = per_off.shape[:-1]
    band = jnp.tile(per_off, (1, 1, r))[..., :r * (p - 1)].reshape(lead + (r, p - 1))[..., :nk]
    return jnp.where(jnp.asarray(valid), band * np.float32(LOG2E), NEG)


def _pool_inverse_counts():
    r = ROWS_MIX
    win = np.repeat(np.asarray(POOL_WINDOWS, np.float32), POOL_GROUP_DIM)[None, :]
    pos = np.arange(1, r + 1, dtype=np.float32)[:, None]
    return jnp.asarray(1.0 / np.minimum(pos, win), F32), jnp.asarray(1.0 / win, F32)


def _block_diag(w):
    nl, g, c, d = w.shape
    eye = jnp.eye(g, dtype=w.dtype)
    return jnp.einsum("gh,lgcd->lgchd", eye, w).reshape(nl, g * c, g * d)


def _prepare(prm):
    nl = prm["w_in"].shape[0]
    row = lambda v: v.reshape(nl, 1, -1).astype(F32)
    hm = _block_diag(jnp.full((1, ATT_HEADS, HEAD_DIM, HEAD_DIM), 1.0 / HEAD_DIM, F32))[0].astype(BF16)
    inproj = dict(
        g=row(prm["mix_norm_g"]), w_in=prm["w_in"].astype(BF16), sgn=row(prm["sgu_norm_g"]),
        gq=row(jnp.tile(prm["q_norm_g"], (1, ATT_HEADS))), gk=row(jnp.tile(prm["k_norm_g"], (1, ATT_HEADS))),
        hm=hm, gb=row(prm["gate_b"]))

    i = np.arange(SGU_BLOCK)
    tri = (i[None, :] // CHUNK) <= (i[:, None] // CHUNK)
    sw = jnp.where(jnp.asarray(tri), prm["sgu_w"], 0.0)
    swc = jnp.concatenate([sw[:, g] for g in range(SGU_GROUPS)], axis=2).astype(BF16)
    sbm = jnp.repeat(jnp.swapaxes(prm["sgu_b"], 1, 2), SGU_GROUP_DIM, axis=2).astype(F32)
    invf, invr = _pool_inverse_counts()
    mixer = dict(invf=invf, invr=invr, bd=_block_diag(prm["pool_w"]).astype(BF16), ps=row(prm["pool_scale"]),
                 swc=swc, sbm=sbm, bias=_attention_bias(prm["rel_bias"]),
                 wa=prm["w_branch_a"].astype(BF16), wb=prm["w_branch_b"].astype(BF16),
                 wc=prm["w_branch_c"].astype(BF16), wo=prm["w_out"].astype(BF16))

    pad = jnp.zeros((nl, SUBLANES - N_GROUPS, D_MODEL), F32)
    wr = jnp.concatenate([jnp.swapaxes(prm["w_group_router"], 1, 2), pad,
                          jnp.swapaxes(prm["w_expert_router"], 1, 2)], axis=1)
    wr_hi = wr.astype(BF16)
    wr_lo = (wr - wr_hi.astype(F32)).astype(BF16)
    br = jnp.concatenate([prm["b_group_router"], jnp.zeros((nl, SUBLANES - N_GROUPS), F32),
                          prm["b_expert_router"]], axis=1).reshape(nl, -1, 1).astype(F32)
    mixer.update(gf=row(prm["ffn_norm_g"]), wr=jnp.concatenate([wr_hi, wr_lo], axis=1), br=br)

    moe = dict(win=prm["w_expert_in"].astype(BF16),
               wout=prm["w_expert_out"].astype(BF16))

    ple = dict(g=row(prm["ple_norm_g"]), wpi=prm["w_ple_in"].astype(BF16), wg=prm["w_ple_gate"].astype(BF16))
    return inproj, mixer, moe, ple


def _sort_by_group(gid, chunk):
    gid = gid.reshape(-1, chunk)
    sidx = jnp.argsort(gid, axis=1, stable=True).astype(I32) * SUBLANES
    cnts = jnp.sum(gid[:, :, None] == jnp.arange(N_GROUPS, dtype=I32)[None, None, :], axis=1).astype(I32)
    offs = jnp.cumsum(cnts, axis=1) - cnts
    return sidx.reshape(-1), offs.reshape(-1).astype(I32), cnts.reshape(-1)


def kernel(x, p, mix_norm_g, w_in, pool_w, pool_scale, sgu_norm_g, sgu_w, sgu_b, q_norm_g, k_norm_g, rel_bias, gate_b, w_branch_a, w_branch_b, w_branch_c, w_out, ffn_norm_g, w_group_router, b_group_router, w_expert_router, b_expert_router, w_expert_in, w_expert_out, ple_norm_g, w_ple_in, w_ple_gate):
    prm = dict(mix_norm_g=mix_norm_g, w_in=w_in, pool_w=pool_w, pool_scale=pool_scale, sgu_norm_g=sgu_norm_g,
               sgu_w=sgu_w, sgu_b=sgu_b, q_norm_g=q_norm_g, k_norm_g=k_norm_g, rel_bias=rel_bias, gate_b=gate_b,
               w_branch_a=w_branch_a, w_branch_b=w_branch_b, w_branch_c=w_branch_c, w_out=w_out,
               ffn_norm_g=ffn_norm_g, w_group_router=w_group_router, b_group_router=b_group_router,
               w_expert_router=w_expert_router, b_expert_router=b_expert_router, w_expert_in=w_expert_in,
               w_expert_out=w_expert_out, ple_norm_g=ple_norm_g, w_ple_in=w_ple_in, w_ple_gate=w_ple_gate)
    b, s, d = x.shape
    t = b * s
    depth = p.shape[0]
    chunk = min(MOE_CHUNK, t)
    assert d == D_MODEL and s % ROWS_MIX == 0 and t % ROWS_PROJ == 0 and t % ROWS_INPROJ == 0 and t % chunk == 0
    assert s // ROWS_MIX >= HALO_BLOCKS + 1
    c_in, c_mix, c_moe, c_ple = _prepare(prm)
    p3d = p.reshape(depth, t, PLE_DIM)
    xf = x.reshape(t, d)
    for l in range(depth):
        a, u, v, q, k, vv, gates = _in_proj(l, xf, c_in)
        xf, hp, gid = _mixer(l, xf, a, u, v, q, k, vv, gates, c_mix, s)
        sidx, offs, cnts = _sort_by_group(gid, chunk)
        mo = _moe(l, hp, sidx, offs, cnts, c_moe, chunk)
        xf = _ple(l, xf, mo, p3d, c_ple)
    return xf.reshape(b, s, d)
```

```python
import functools

import jax
import jax.numpy as jnp
import numpy as np
from jax import lax
from jax.experimental import pallas as pl
from jax.experimental.pallas import tpu as pltpu

F32 = jnp.float32
BF16 = jnp.bfloat16
U32 = jnp.uint32
I32 = jnp.int32

D_MODEL = 1024
CHUNK = 64
PLE_DIM = 256
EPS = 1e-6
POOL_WINDOWS = (2, 4, 8, 16)
POOL_WIDTH = 256
POOL_GROUP_DIM = 64
SGU_BLOCK = 128
SGU_WIDTH = 256
SGU_GROUPS = 4
SGU_GROUP_DIM = 64
HEAD_DIM = 64
ATT_WIDTH = 512
ATT_HEADS = 8
LEFT_CHUNKS = 8
REL_CLIP = 128
N_BRANCH = 3
N_GROUPS = 4
EXPERTS_PER_GROUP = 8
D_EXPERT = 128
GROUP_FF = EXPERTS_PER_GROUP * D_EXPERT

C_A, C_U, C_V, C_Q, C_K, C_VV, C_G = 0, 256, 512, 768, 1280, 1792, 2304

LANES = 128
SUBLANES = 8
VMEM_LIMIT = 56 * 1024 * 1024

ROWS_INPROJ = 512
ROWS_PROJ = 512
ROWS_MIX = 256
HALO_BLOCKS = 2
POOL_HALO = 16
MOE_CHUNK = 2048
MOE_TILE = 256
MOE_STRIDE = MOE_TILE + 1
MOE_UNROLL = 32
PACK_W = D_MODEL // 2
PACK_ROWS = PACK_W // LANES
NEG = -1e30
LOG2E = float(np.log2(np.e))


def _const_spec(shape):
    nd = len(shape)
    return pl.BlockSpec(shape, lambda *_: (0,) * nd, pipeline_mode=pl.Buffered(1))


def _layer_spec(arr, l):
    nd = arr.ndim - 1
    return pl.BlockSpec((None,) + arr.shape[1:], lambda *_: (l,) + (0,) * nd, pipeline_mode=pl.Buffered(1))


def _params(n_axes):
    return pltpu.CompilerParams(dimension_semantics=("arbitrary",) * n_axes,
                                vmem_limit_bytes=VMEM_LIMIT)


def _rms(x, g):
    ms = jnp.mean(x * x, axis=-1, keepdims=True)
    return x * lax.rsqrt(ms + EPS) * g


def _gelu_tanh(x):
    c = np.float32(np.sqrt(2.0 / np.pi))
    return 0.5 * x * (1.0 + jnp.tanh(c * (x + np.float32(0.044715) * (x * x * x))))


def _sigmoid(x):
    return 0.5 * jnp.tanh(0.5 * x) + 0.5


def _dot(a, b):
    return jnp.dot(a, b, preferred_element_type=F32)


def _dot_nt(a, b):
    return lax.dot_general(a, b, (((1,), (1,)), ((), ())), preferred_element_type=F32)


def _pack_pairs(x):
    w = x.shape[1] // 2
    bits = pltpu.bitcast(x.astype(BF16).astype(F32), U32)
    return (bits[:, :w] >> 16) | (bits[:, w:] & np.uint32(0xFFFF0000))


def _unpack_pairs(p):
    lo = pltpu.bitcast(p << 16, F32)
    hi = pltpu.bitcast(p & np.uint32(0xFFFF0000), F32)
    return jnp.concatenate([lo, hi], axis=1)


def _slab_rows(j, n):
    return pl.ds(j, n, stride=SUBLANES)


def _inproj_kernel(x_ref, g_ref, w_ref, sgn_ref, gq_ref, gk_ref, hm_ref, gb_ref,
                   a_ref, u_ref, v_ref, q_ref, k_ref, vv_ref, gates_ref):
    h = _rms(x_ref[...], g_ref[...]).astype(BF16)

    def seg(c0, c1):
        return _dot(h, w_ref[:, c0:c1])

    def head_norm(z, g):
        ms = _dot((z * z).astype(BF16), hm_ref[...])
        return z * lax.rsqrt(ms + EPS) * g

    a_ref[...] = seg(C_A, C_U).astype(BF16)
    u_ref[...] = _gelu_tanh(seg(C_U, C_V)).astype(BF16)
    v_ref[...] = _rms(_gelu_tanh(seg(C_V, C_Q)), sgn_ref[...]).astype(BF16)
    q_ref[...] = (head_norm(seg(C_Q, C_K), gq_ref[...]) * np.float32(HEAD_DIM ** -0.5 * LOG2E)).astype(BF16)
    k_ref[...] = head_norm(seg(C_K, C_VV), gk_ref[...]).astype(BF16)
    vv_ref[...] = seg(C_VV, C_G).astype(BF16)
    step = 512
    for c0 in range(0, N_BRANCH * D_MODEL, step):
        z = seg(C_G + c0, C_G + c0 + step) + gb_ref[:, c0:c0 + step]
        gates_ref[:, c0:c0 + step] = _sigmoid(z).astype(BF16)


def _in_proj(l, x2d, c):
    t = x2d.shape[0]
    tm = ROWS_INPROJ
    row = lambda w: pl.BlockSpec((tm, w), lambda i: (i, 0))
    widths = (POOL_WIDTH, SGU_WIDTH, SGU_WIDTH, ATT_WIDTH, ATT_WIDTH, ATT_WIDTH, N_BRANCH * D_MODEL)
    layer = [c["g"], c["w_in"], c["sgn"], c["gq"], c["gk"]]
    return pl.pallas_call(
        _inproj_kernel,
        grid=(t // tm,),
        in_specs=[row(D_MODEL)] + [_layer_spec(w, l) for w in layer]
                 + [_const_spec(c["hm"].shape), _layer_spec(c["gb"], l)],
        out_specs=[row(w) for w in widths],
        out_shape=[jax.ShapeDtypeStruct((t, w), BF16) for w in widths],
        compiler_params=_params(1),
        name="in_proj",
    )(x2d, *layer, c["hm"], c["gb"])


def _mixer_kernel(blocks_per_seq,
                  x_ref, a_ref, ap_ref, u_ref, v_ref, q_ref,
                  k0_ref, k1_ref, k2_ref, v0_ref, v1_ref, v2_ref, gates_ref,
                  invf_ref, invr_ref, bd_ref, ps_ref, swc_ref, sb_ref, bias_ref,
                  wa_ref, wb_ref, wc_ref, wo_ref, gf_ref, wr_ref, br_ref,
                  o_ref, hp_ref, gid_ref, pool_buf, x1_buf):
    r = ROWS_MIX
    step = pl.program_id(0)
    blk = jnp.minimum(step, pl.num_programs(0) - 2)
    bi = blk % blocks_per_seq
    first = bi == 0

    @pl.when(step == 0)
    def _():
        x1_buf[...] = jnp.zeros_like(x1_buf)

    a = a_ref[...].astype(F32)
    halo = jnp.where(first, 0.0, ap_ref[r - POOL_HALO:, :].astype(F32))
    pool_buf[0:POOL_HALO, :] = jnp.zeros((POOL_HALO, POOL_WIDTH), F32)
    pool_buf[POOL_HALO:2 * POOL_HALO, :] = halo
    pool_buf[2 * POOL_HALO:, :] = a
    lane_a = lax.broadcasted_iota(I32, (r, POOL_WIDTH), 1) // POOL_GROUP_DIM
    wsum = None
    for gi, k in enumerate((1, 2, 4, 8)):
        cur = pool_buf[POOL_HALO:, :] + pool_buf[POOL_HALO - k:r + 2 * POOL_HALO - k, :]
        pool_buf[POOL_HALO:, :] = cur
        level = cur[POOL_HALO:, :]
        wsum = level if wsum is None else jnp.where(lane_a >= gi, level, wsum)
    inv = jnp.where(first, invf_ref[...], invr_ref[...])
    pooled = (wsum * inv - a).astype(BF16)
    ya = (_dot(pooled, bd_ref[...]) * ps_ref[...]).astype(BF16)

    vb = v_ref[...]
    lane_b = lax.broadcasted_iota(I32, (SGU_BLOCK, SGU_WIDTH), 1) // SGU_GROUP_DIM
    zero_b = jnp.zeros((SGU_BLOCK, SGU_WIDTH), BF16)
    mixed = []
    for s in range(r // SGU_BLOCK):
        blk = vb[s * SGU_BLOCK:(s + 1) * SGU_BLOCK, :]
        rhs = jnp.concatenate([jnp.where(lane_b == g, blk, zero_b) for g in range(SGU_GROUPS)], axis=0)
        mixed.append(_dot(swc_ref[...], rhs) + sb_ref[...])
    yb = (u_ref[...].astype(F32) * jnp.concatenate(mixed, axis=0)).astype(BF16)

    lane_c = lax.broadcasted_iota(I32, (r, LANES), 1)
    zero_q = jnp.zeros((r, LANES), BF16)
    k_refs = (k0_ref, k1_ref, k2_ref)
    v_refs = (v0_ref, v1_ref, v2_ref)
    blk_ok = (bi >= 2, bi >= 1, None)
    nblk = HALO_BLOCKS + 1
    yc = []
    for pr in range(ATT_HEADS // 2):
        cols = slice(pr * LANES, (pr + 1) * LANES)
        qp = q_ref[:, cols]
        qs = jnp.concatenate([jnp.where(lane_c < HEAD_DIM, qp, zero_q),
                              jnp.where(lane_c >= HEAD_DIM, qp, zero_q)], axis=0)
        kcat = jnp.concatenate([kr[:, cols] for kr in k_refs], axis=0)
        vcat = jnp.concatenate([vr[:, cols] for vr in v_refs], axis=0)
        bias = []
        for j in range(nblk):
            bj = bias_ref[2 * pr:2 * pr + 2, :, j * r:(j + 1) * r].reshape(2 * r, r)
            bias.append(bj if blk_ok[j] is None else jnp.where(blk_ok[j], bj, NEG))
        sc = _dot_nt(qs, kcat) + jnp.concatenate(bias, axis=1)
        m = jnp.max(sc, axis=-1, keepdims=True)
        p = jnp.exp2(sc - m)
        denom = jnp.sum(p, axis=-1, keepdims=True)
        o = _dot(p.astype(BF16), vcat) / denom
        yc.append(jnp.where(lane_c < HEAD_DIM, o[0:r], o[r:2 * r]))
    yc = jnp.concatenate(yc, axis=1).astype(BF16)

    _route(x1_buf[...], gf_ref, wr_ref, br_ref, hp_ref, gid_ref)

    d = D_MODEL
    merged = (gates_ref[:, 0:d].astype(F32) * _dot(ya, wa_ref[...])
              + gates_ref[:, d:2 * d].astype(F32) * _dot(yb, wb_ref[...])
              + gates_ref[:, 2 * d:3 * d].astype(F32) * _dot(yc, wc_ref[...]))
    x1 = x_ref[...] + _dot(merged.astype(BF16), wo_ref[...])
    o_ref[...] = x1
    x1_buf[...] = x1


def _mixer(l, x2d, a, u, v, q, k, vv, gates, c, seq):
    t = x2d.shape[0]
    r = ROWS_MIX
    nb = t // r
    prev = lambda w, n: pl.BlockSpec((r, w), lambda i: (jnp.maximum(jnp.minimum(i, nb - 1) - n, 0), 0))
    row = lambda w: prev(w, 0)
    routed = lambda shape: pl.BlockSpec(shape, lambda i: (jnp.maximum(i - 1, 0),) + (0,) * (len(shape) - 1))
    layer = [c["bd"], c["ps"], c["swc"], c["sbm"], c["bias"], c["wa"], c["wb"], c["wc"], c["wo"],
             c["gf"], c["wr"], c["br"]]
    in_specs = [row(D_MODEL), row(POOL_WIDTH), prev(POOL_WIDTH, 1), row(SGU_WIDTH), row(SGU_WIDTH),
                row(ATT_WIDTH),
                prev(ATT_WIDTH, 2), prev(ATT_WIDTH, 1), row(ATT_WIDTH),
                prev(ATT_WIDTH, 2), prev(ATT_WIDTH, 1), row(ATT_WIDTH),
                row(N_BRANCH * D_MODEL),
                _const_spec(c["invf"].shape), _const_spec(c["invr"].shape)] + [_layer_spec(w, l) for w in layer]
    return pl.pallas_call(
        functools.partial(_mixer_kernel, seq // r),
        grid=(nb + 1,),
        in_specs=in_specs,
        out_specs=[row(D_MODEL), routed((r * SUBLANES, LANES)), routed((1, 1, r))],
        out_shape=[jax.ShapeDtypeStruct((t, D_MODEL), F32), jax.ShapeDtypeStruct((t * SUBLANES, LANES), U32),
                   jax.ShapeDtypeStruct((nb, 1, r), I32)],
        scratch_shapes=[pltpu.VMEM((r + 2 * POOL_HALO, POOL_WIDTH), F32),
                        pltpu.VMEM((r, D_MODEL), F32)],
        compiler_params=_params(1),
        name="mixer",
    )(x2d, a, a, u, v, q, k, k, k, vv, vv, vv, gates, c["invf"], c["invr"], *layer)


ROUTER_ROWS = SUBLANES + N_GROUPS * EXPERTS_PER_GROUP


def _route(x, g_ref, wr_ref, br_ref, hp_ref, gid_ref):
    tm = x.shape[0]
    h = _rms(x, g_ref[...])
    h_hi = h.astype(BF16)
    h_lo = (h - h_hi.astype(F32)).astype(BF16)
    nr = ROUTER_ROWS
    l1 = _dot_nt(wr_ref[...], h_hi)
    l2 = _dot_nt(wr_ref[0:nr, :], h_lo)
    logits = l1[0:nr] + l1[nr:2 * nr] + l2 + br_ref[...]

    rows = lax.broadcasted_iota(I32, (SUBLANES, tm), 0).astype(F32)
    big = np.float32(1 << 20)
    gl = jnp.where(rows < N_GROUPS, logits[0:SUBLANES], NEG)
    gmax = jnp.max(gl, axis=0, keepdims=True)
    g_p = 1.0 / jnp.sum(jnp.exp(gl - gmax), axis=0, keepdims=True)
    g_idx = jnp.min(jnp.where(gl == gmax, rows, big), axis=0, keepdims=True)

    el = logits[SUBLANES:2 * SUBLANES]
    for gi in range(1, N_GROUPS):
        el = jnp.where(g_idx == gi, logits[SUBLANES * (gi + 1):SUBLANES * (gi + 2)], el)
    emax = jnp.max(el, axis=0, keepdims=True)
    esum = jnp.sum(jnp.exp(el - emax), axis=0, keepdims=True)
    i0 = jnp.min(jnp.where(el == emax, rows, big), axis=0, keepdims=True)
    el2 = jnp.where(rows == i0, -jnp.inf, el)
    m2 = jnp.max(el2, axis=0, keepdims=True)
    i1 = jnp.min(jnp.where(el2 == m2, rows, big), axis=0, keepdims=True)
    p0 = 1.0 / esum
    p1 = jnp.exp(m2 - emax) / esum
    w0 = g_p * p0 / (p0 + p1)
    w1 = g_p * p1 / (p0 + p1)
    gates = jnp.where(rows == i0, w0, 0.0) + jnp.where(rows == i1, w1, 0.0)

    gates_t = jnp.concatenate([gates, jnp.zeros((LANES - SUBLANES, tm), F32)], axis=0).T
    packed = _pack_pairs(h)
    for j in range(PACK_ROWS):
        hp_ref[_slab_rows(j, tm), :] = packed[:, j * LANES:(j + 1) * LANES]
    hp_ref[_slab_rows(PACK_ROWS, tm), :] = pltpu.bitcast(gates_t, U32)
    for j in range(PACK_ROWS + 1, SUBLANES):
        hp_ref[_slab_rows(j, tm), :] = jnp.zeros((tm, LANES), U32)
    gid_ref[...] = g_idx.astype(I32).reshape(1, 1, tm)


def _moe_kernel(chunk, sidx_ref, offs_ref, cnts_ref, hp_ref, win_ref, wout_ref, o_ref, xt_ref, yt_ref):
    c = pl.program_id(0)
    g = pl.program_id(1)
    off = offs_ref[c * N_GROUPS + g]
    n = cnts_ref[c * N_GROUPS + g]
    base = c * chunk
    last = off + n - 1
    tr = MOE_TILE
    st = MOE_STRIDE

    def tile_body(t, carry):
        r0 = off + t * tr

        def slab(rr):
            start = pl.multiple_of(sidx_ref[base + jnp.minimum(r0 + rr, last)], SUBLANES)
            return pl.ds(start, SUBLANES)

        def gather(i, cr):
            for uu in range(MOE_UNROLL):
                rr = i * MOE_UNROLL + uu
                xt_ref[pl.ds(rr, SUBLANES, stride=st), :] = hp_ref[slab(rr), :]
            return cr

        lax.fori_loop(0, tr // MOE_UNROLL, gather, 0)

        packed = jnp.concatenate([xt_ref[pl.ds(j * st, tr), :] for j in range(PACK_ROWS)], axis=1)
        xb = _unpack_pairs(packed).astype(BF16)
        gates = pltpu.bitcast(xt_ref[pl.ds(PACK_ROWS * st, tr), :], F32)
        acts = []
        for e in range(EXPERTS_PER_GROUP):
            hu = _dot(xb, win_ref[e])
            hg = hu[:, 0:D_EXPERT]
            acts.append((hg * _sigmoid(hg) * hu[:, D_EXPERT:] * gates[:, e:e + 1]).astype(BF16))
        y = _dot(jnp.concatenate(acts, axis=1), wout_ref[...].reshape(GROUP_FF, D_MODEL))
        for j in range(SUBLANES):
            yt_ref[pl.ds(j * st, tr), :] = y[:, j * LANES:(j + 1) * LANES]

        def scatter(i, cr):
            for uu in range(MOE_UNROLL):
                rr = i * MOE_UNROLL + uu
                o_ref[slab(rr), :] = yt_ref[pl.ds(rr, SUBLANES, stride=st), :]
            return cr

        lax.fori_loop(0, tr // MOE_UNROLL, scatter, 0)
        return carry

    lax.fori_loop(0, (n + tr - 1) // tr, tile_body, 0)


def _moe(l, hp, sidx, offs, cnts, c, chunk):
    t = hp.shape[0] // SUBLANES
    win, wout = c["win"], c["wout"]
    grid_spec = pltpu.PrefetchScalarGridSpec(
        num_scalar_prefetch=3,
        grid=(t // chunk, N_GROUPS),
        in_specs=[pl.BlockSpec((chunk * SUBLANES, LANES), lambda ci, g, *_: (ci, 0)),
                  pl.BlockSpec((None, EXPERTS_PER_GROUP) + win.shape[2:], lambda ci, g, *_: (l, g, 0, 0)),
                  pl.BlockSpec((None, EXPERTS_PER_GROUP) + wout.shape[2:], lambda ci, g, *_: (l, g, 0, 0))],
        out_specs=pl.BlockSpec((chunk * SUBLANES, LANES), lambda ci, g, *_: (ci, 0)),
        scratch_shapes=[pltpu.VMEM((MOE_STRIDE * SUBLANES, LANES), U32),
                        pltpu.VMEM((MOE_STRIDE * SUBLANES, LANES), F32)],
    )
    return pl.pallas_call(
        functools.partial(_moe_kernel, chunk),
        grid_spec=grid_spec,
        out_shape=jax.ShapeDtypeStruct((t * SUBLANES, LANES), F32),
        compiler_params=_params(2),
        name="moe",
    )(sidx, offs, cnts, hp, win, wout)


def _ple_kernel(x_ref, mo_ref, p_ref, g_ref, wpi_ref, wg_ref, o_ref):
    tm = x_ref.shape[0]
    mo = jnp.concatenate([mo_ref[_slab_rows(j, tm), :] for j in range(SUBLANES)], axis=1)
    x = x_ref[...] + mo
    hn = _rms(x, g_ref[...]).astype(BF16)
    gate = _sigmoid(_dot(hn, wg_ref[...]))
    emb = _dot(p_ref[...].astype(BF16), wpi_ref[...])
    o_ref[...] = x + emb * gate


def _ple(l, x2d, mo, p3d, c):
    t = x2d.shape[0]
    tm = ROWS_PROJ
    row = lambda w: pl.BlockSpec((tm, w), lambda i: (i, 0))
    layer = [c["g"], c["wpi"], c["wg"]]
    return pl.pallas_call(
        _ple_kernel,
        grid=(t // tm,),
        in_specs=[row(D_MODEL), pl.BlockSpec((tm * SUBLANES, LANES), lambda i: (i, 0)),
                  pl.BlockSpec((None, tm, PLE_DIM), lambda i: (l, i, 0))] + [_layer_spec(w, l) for w in layer],
        out_specs=row(D_MODEL),
        out_shape=jax.ShapeDtypeStruct((t, D_MODEL), F32),
        compiler_params=_params(1),
        name="ple",
    )(x2d, mo, p3d, *layer)


def _attention_bias(rel_bias):
    r = ROWS_MIX
    nk = (HALO_BLOCKS + 1) * r
    qi = np.arange(r)[:, None]
    kj = np.arange(nk)[None, :]
    qc = qi // CHUNK
    kc = kj // CHUNK
    valid = (kc >= qc) & (kc <= qc + LEFT_CHUNKS)
    p = nk + r
    off = np.arange(p)
    off = np.where(off < nk, off, off - p)
    rel = np.clip(HALO_BLOCKS * r - off, -REL_CLIP, REL_CLIP) + REL_CLIP
    per_off = rel_bias[..., rel].astype(F32)
    lead = per_off.shape[:-1]
    band = jnp.tile(per_off, (1, 1, r))[..., :r * (p - 1)].reshape(lead + (r, p - 1))[..., :nk]
    return jnp.where(jnp.asarray(valid), band * np.float32(LOG2E), NEG)


def _pool_inverse_counts():
    r = ROWS_MIX
    win = np.repeat(np.asarray(POOL_WINDOWS, np.float32), POOL_GROUP_DIM)[None, :]
    pos = np.arange(1, r + 1, dtype=np.float32)[:, None]
    return jnp.asarray(1.0 / np.minimum(pos, win), F32), jnp.asarray(1.0 / win, F32)


def _block_diag(w):
    nl, g, c, d = w.shape
    eye = jnp.eye(g, dtype=w.dtype)
    return jnp.einsum("gh,lgcd->lgchd", eye, w).reshape(nl, g * c, g * d)


def _prepare(prm):
    nl = prm["w_in"].shape[0]
    row = lambda v: v.reshape(nl, 1, -1).astype(F32)
    hm = _block_diag(jnp.full((1, ATT_HEADS, HEAD_DIM, HEAD_DIM), 1.0 / HEAD_DIM, F32))[0].astype(BF16)
    inproj = dict(
        g=row(prm["mix_norm_g"]), w_in=prm["w_in"].astype(BF16), sgn=row(prm["sgu_norm_g"]),
        gq=row(jnp.tile(prm["q_norm_g"], (1, ATT_HEADS))), gk=row(jnp.tile(prm["k_norm_g"], (1, ATT_HEADS))),
        hm=hm, gb=row(prm["gate_b"]))

    i = np.arange(SGU_BLOCK)
    tri = (i[None, :] // CHUNK) <= (i[:, None] // CHUNK)
    sw = jnp.where(jnp.asarray(tri), prm["sgu_w"], 0.0)
    swc = jnp.concatenate([sw[:, g] for g in range(SGU_GROUPS)], axis=2).astype(BF16)
    sbm = jnp.repeat(jnp.swapaxes(prm["sgu_b"], 1, 2), SGU_GROUP_DIM, axis=2).astype(F32)
    invf, invr = _pool_inverse_counts()
    mixer = dict(invf=invf, invr=invr, bd=_block_diag(prm["pool_w"]).astype(BF16), ps=row(prm["pool_scale"]),
                 swc=swc, sbm=sbm, bias=_attention_bias(prm["rel_bias"]),
                 wa=prm["w_branch_a"].astype(BF16), wb=prm["w_branch_b"].astype(BF16),
                 wc=prm["w_branch_c"].astype(BF16), wo=prm["w_out"].astype(BF16))

    pad = jnp.zeros((nl, SUBLANES - N_GROUPS, D_MODEL), F32)
    wr = jnp.concatenate([jnp.swapaxes(prm["w_group_router"], 1, 2), pad,
                          jnp.swapaxes(prm["w_expert_router"], 1, 2)], axis=1)
    wr_hi = wr.astype(BF16)
    wr_lo = (wr - wr_hi.astype(F32)).astype(BF16)
    br = jnp.concatenate([prm["b_group_router"], jnp.zeros((nl, SUBLANES - N_GROUPS), F32),
                          prm["b_expert_router"]], axis=1).reshape(nl, -1, 1).astype(F32)
    mixer.update(gf=row(prm["ffn_norm_g"]), wr=jnp.concatenate([wr_hi, wr_lo], axis=1), br=br)

    moe = dict(win=prm["w_expert_in"].astype(BF16),
               wout=prm["w_expert_out"].astype(BF16))

    ple = dict(g=row(prm["ple_norm_g"]), wpi=prm["w_ple_in"].astype(BF16), wg=prm["w_ple_gate"].astype(BF16))
    return inproj, mixer, moe, ple


def _sort_by_group(gid, chunk):
    gid = gid.reshape(-1, chunk)
    sidx = jnp.argsort(gid, axis=1, stable=True).astype(I32) * SUBLANES
    cnts = jnp.sum(gid[:, :, None] == jnp.arange(N_GROUPS, dtype=I32)[None, None, :], axis=1).astype(I32)
    offs = jnp.cumsum(cnts, axis=1) - cnts
    return sidx.reshape(-1), offs.reshape(-1).astype(I32), cnts.reshape(-1)


def kernel(x, p, mix_norm_g, w_in, pool_w, pool_scale, sgu_norm_g, sgu_w, sgu_b, q_norm_g, k_norm_g, rel_bias, gate_b, w_branch_a, w_branch_b, w_branch_c, w_out, ffn_norm_g, w_group_router, b_group_router, w_expert_router, b_expert_router, w_expert_in, w_expert_out, ple_norm_g, w_ple_in, w_ple_gate):
    prm = dict(mix_norm_g=mix_norm_g, w_in=w_in, pool_w=pool_w, pool_scale=pool_scale, sgu_norm_g=sgu_norm_g,
               sgu_w=sgu_w, sgu_b=sgu_b, q_norm_g=q_norm_g, k_norm_g=k_norm_g, rel_bias=rel_bias, gate_b=gate_b,
               w_branch_a=w_branch_a, w_branch_b=w_branch_b, w_branch_c=w_branch_c, w_out=w_out,
               ffn_norm_g=ffn_norm_g, w_group_router=w_group_router, b_group_router=b_group_router,
               w_expert_router=w_expert_router, b_expert_router=b_expert_router, w_expert_in=w_expert_in,
               w_expert_out=w_expert_out, ple_norm_g=ple_norm_g, w_ple_in=w_ple_in, w_ple_gate=w_ple_gate)
    b, s, d = x.shape
    t = b * s
    depth = p.shape[0]
    chunk = min(MOE_CHUNK, t)
    assert d == D_MODEL and s % ROWS_MIX == 0 and t % ROWS_PROJ == 0 and t % ROWS_INPROJ == 0 and t % chunk == 0
    assert s // ROWS_MIX >= HALO_BLOCKS + 1
    c_in, c_mix, c_moe, c_ple = _prepare(prm)
    p3d = p.reshape(depth, t, PLE_DIM)
    xf = x.reshape(t, d)
    for l in range(depth):
        a, u, v, q, k, vv, gates = _in_proj(l, xf, c_in)
        xf, hp, gid = _mixer(l, xf, a, u, v, q, k, vv, gates, c_mix, s)
        sidx, offs, cnts = _sort_by_group(gid, chunk)
        mo = _moe(l, hp, sidx, offs, cnts, c_moe, chunk)
        xf = _ple(l, xf, mo, p3d, c_ple)
    return xf.reshape(b, s, d)
```

```python
import functools

import jax
import jax.numpy as jnp
import numpy as np
from jax import lax
from jax.experimental import pallas as pl
from jax.experimental.pallas import tpu as pltpu

F32 = jnp.float32
BF16 = jnp.bfloat16
U32 = jnp.uint32
I32 = jnp.int32

D_MODEL = 1024
CHUNK = 64
PLE_DIM = 256
EPS = 1e-6
POOL_WINDOWS = (2, 4, 8, 16)
POOL_WIDTH = 256
POOL_GROUP_DIM = 64
SGU_BLOCK = 128
SGU_WIDTH = 256
SGU_GROUPS = 4
SGU_GROUP_DIM = 64
HEAD_DIM = 64
ATT_WIDTH = 512
ATT_HEADS = 8
LEFT_CHUNKS = 8
REL_CLIP = 128
N_BRANCH = 3
N_GROUPS = 4
EXPERTS_PER_GROUP = 8
D_EXPERT = 128
GROUP_FF = EXPERTS_PER_GROUP * D_EXPERT

C_A, C_U, C_V, C_Q, C_K, C_VV, C_G = 0, 256, 512, 768, 1280, 1792, 2304

LANES = 128
SUBLANES = 8
MXU_DIM = 256
VMEM_LIMIT = 56 * 1024 * 1024

ROWS_INPROJ = 512
ROWS_PROJ = 512
ROWS_MIX = 256
HALO_BLOCKS = 2
POOL_HALO = 16
MOE_CHUNK = 2048
MOE_TILE = 256
MOE_STRIDE = MOE_TILE + 1
MOE_UNROLL = 32
PACK_W = D_MODEL // 2
PACK_ROWS = PACK_W // LANES
NEG = -1e30
LOG2E = float(np.log2(np.e))


def _const_spec(shape):
    nd = len(shape)
    return pl.BlockSpec(shape, lambda *_: (0,) * nd, pipeline_mode=pl.Buffered(1))


def _layer_spec(arr, l):
    nd = arr.ndim - 1
    return pl.BlockSpec((None,) + arr.shape[1:], lambda *_: (l,) + (0,) * nd, pipeline_mode=pl.Buffered(1))


def _params(n_axes):
    return pltpu.CompilerParams(dimension_semantics=("arbitrary",) * n_axes,
                                vmem_limit_bytes=VMEM_LIMIT)


def _rms(x, g):
    ms = jnp.mean(x * x, axis=-1, keepdims=True)
    return x * lax.rsqrt(ms + EPS) * g


def _gelu_tanh(x):
    c = np.float32(np.sqrt(2.0 / np.pi))
    return 0.5 * x * (1.0 + jnp.tanh(c * (x + np.float32(0.044715) * (x * x * x))))


def _sigmoid(x):
    return 0.5 * jnp.tanh(0.5 * x) + 0.5


def _dot(a, b):
    return jnp.dot(a, b, preferred_element_type=F32)


def _dot_nt(a, b):
    return lax.dot_general(a, b, (((1,), (1,)), ((), ())), preferred_element_type=F32)


def _pack_pairs(x):
    w = x.shape[1] // 2
    bits = pltpu.bitcast(x.astype(BF16).astype(F32), U32)
    return (bits[:, :w] >> 16) | (bits[:, w:] & np.uint32(0xFFFF0000))


def _unpack_pairs(p):
    lo = pltpu.bitcast(p << 16, F32)
    hi = pltpu.bitcast(p & np.uint32(0xFFFF0000), F32)
    return jnp.concatenate([lo, hi], axis=1)


def _slab_rows(j, n):
    return pl.ds(j, n, stride=SUBLANES)


def _inproj_kernel(x_ref, g_ref, w_ref, sgn_ref, gq_ref, gk_ref, hm_ref, gb_ref,
                   a_ref, u_ref, v_ref, q_ref, k_ref, vv_ref, gates_ref):
    h = _rms(x_ref[...], g_ref[...]).astype(BF16)

    def seg(c0, c1):
        return _dot(h, w_ref[:, c0:c1])

    def head_norm(z, g):
        zz = (z * z).astype(BF16)
        w = hm_ref.shape[0]
        ms = jnp.concatenate([_dot(zz[:, c0:c0 + w], hm_ref[...]) for c0 in range(0, ATT_WIDTH, w)], axis=1)
        return z * lax.rsqrt(ms + EPS) * g

    a_ref[...] = seg(C_A, C_U).astype(BF16)
    u_ref[...] = _gelu_tanh(seg(C_U, C_V)).astype(BF16)
    v_ref[...] = _rms(_gelu_tanh(seg(C_V, C_Q)), sgn_ref[...]).astype(BF16)
    q_ref[...] = (head_norm(seg(C_Q, C_K), gq_ref[...]) * np.float32(HEAD_DIM ** -0.5 * LOG2E)).astype(BF16)
    k_ref[...] = head_norm(seg(C_K, C_VV), gk_ref[...]).astype(BF16)
    vv_ref[...] = seg(C_VV, C_G).astype(BF16)
    step = 512
    for c0 in range(0, N_BRANCH * D_MODEL, step):
        z = seg(C_G + c0, C_G + c0 + step) + gb_ref[:, c0:c0 + step]
        gates_ref[:, c0:c0 + step] = _sigmoid(z).astype(BF16)


def _in_proj(l, x2d, c):
    t = x2d.shape[0]
    tm = ROWS_INPROJ
    row = lambda w: pl.BlockSpec((tm, w), lambda i: (i, 0))
    widths = (POOL_WIDTH, SGU_WIDTH, SGU_WIDTH, ATT_WIDTH, ATT_WIDTH, ATT_WIDTH, N_BRANCH * D_MODEL)
    layer = [c["g"], c["w_in"], c["sgn"], c["gq"], c["gk"]]
    return pl.pallas_call(
        _inproj_kernel,
        grid=(t // tm,),
        in_specs=[row(D_MODEL)] + [_layer_spec(w, l) for w in layer]
                 + [_const_spec(c["hm"].shape), _layer_spec(c["gb"], l)],
        out_specs=[row(w) for w in widths],
        out_shape=[jax.ShapeDtypeStruct((t, w), BF16) for w in widths],
        compiler_params=_params(1),
        name="in_proj",
    )(x2d, *layer, c["hm"], c["gb"])


def _mixer_kernel(blocks_per_seq,
                  x_ref, a_ref, ap_ref, u_ref, v_ref, q_ref,
                  k0_ref, k1_ref, k2_ref, v0_ref, v1_ref, v2_ref, gates_ref,
                  invf_ref, invr_ref, bd_ref, ps_ref, swc_ref, sb_ref, bias_ref,
                  wa_ref, wb_ref, wc_ref, wo_ref, gf_ref, wr_ref, br_ref,
                  o_ref, hp_ref, gid_ref, pool_buf, x1_buf):
    r = ROWS_MIX
    step = pl.program_id(0)
    blk = jnp.minimum(step, pl.num_programs(0) - 2)
    bi = blk % blocks_per_seq
    first = bi == 0

    @pl.when(step == 0)
    def _():
        x1_buf[...] = jnp.zeros_like(x1_buf)

    a = a_ref[...].astype(F32)
    halo = jnp.where(first, 0.0, ap_ref[r - POOL_HALO:, :].astype(F32))
    pool_buf[0:POOL_HALO, :] = jnp.zeros((POOL_HALO, POOL_WIDTH), F32)
    pool_buf[POOL_HALO:2 * POOL_HALO, :] = halo
    pool_buf[2 * POOL_HALO:, :] = a
    lane_a = lax.broadcasted_iota(I32, (r, POOL_WIDTH), 1) // POOL_GROUP_DIM
    wsum = None
    for gi, k in enumerate((1, 2, 4, 8)):
        cur = pool_buf[POOL_HALO:, :] + pool_buf[POOL_HALO - k:r + 2 * POOL_HALO - k, :]
        pool_buf[POOL_HALO:, :] = cur
        level = cur[POOL_HALO:, :]
        wsum = level if wsum is None else jnp.where(lane_a >= gi, level, wsum)
    inv = jnp.where(first, invf_ref[...], invr_ref[...])
    pooled = (wsum * inv - a).astype(BF16)
    ya = (_dot(pooled, bd_ref[...]) * ps_ref[...]).astype(BF16)

    vb = v_ref[...]
    lane_b = lax.broadcasted_iota(I32, (SGU_BLOCK, SGU_WIDTH), 1) // SGU_GROUP_DIM
    zero_b = jnp.zeros((SGU_BLOCK, SGU_WIDTH), BF16)
    mixed = []
    for s in range(r // SGU_BLOCK):
        blk = vb[s * SGU_BLOCK:(s + 1) * SGU_BLOCK, :]
        rhs = jnp.concatenate([jnp.where(lane_b == g, blk, zero_b) for g in range(SGU_GROUPS)], axis=0)
        mixed.append(_dot(swc_ref[...], rhs) + sb_ref[...])
    yb = (u_ref[...].astype(F32) * jnp.concatenate(mixed, axis=0)).astype(BF16)

    lane_c = lax.broadcasted_iota(I32, (r, LANES), 1)
    zero_q = jnp.zeros((r, LANES), BF16)
    k_refs = (k0_ref, k1_ref, k2_ref)
    v_refs = (v0_ref, v1_ref, v2_ref)
    blk_ok = (bi >= 2, bi >= 1, None)
    nblk = HALO_BLOCKS + 1
    yc = []
    for pr in range(ATT_HEADS // 2):
        cols = slice(pr * LANES, (pr + 1) * LANES)
        qp = q_ref[:, cols]
        qs = jnp.concatenate([jnp.where(lane_c < HEAD_DIM, qp, zero_q),
                              jnp.where(lane_c >= HEAD_DIM, qp, zero_q)], axis=0)
        kcat = jnp.concatenate([kr[:, cols] for kr in k_refs], axis=0)
        vcat = jnp.concatenate([vr[:, cols] for vr in v_refs], axis=0)
        bias = []
        for j in range(nblk):
            bj = bias_ref[2 * pr:2 * pr + 2, :, j * r:(j + 1) * r].reshape(2 * r, r)
            bias.append(bj if blk_ok[j] is None else jnp.where(blk_ok[j], bj, NEG))
        sc = _dot_nt(qs, kcat) + jnp.concatenate(bias, axis=1)
        m = jnp.max(sc, axis=-1, keepdims=True)
        p = jnp.exp2(sc - m)
        denom = jnp.sum(p, axis=-1, keepdims=True)
        o = _dot(p.astype(BF16), vcat) / denom
        yc.append(jnp.where(lane_c < HEAD_DIM, o[0:r], o[r:2 * r]))
    yc = jnp.concatenate(yc, axis=1).astype(BF16)

    _route(x1_buf[...], gf_ref, wr_ref, br_ref, hp_ref, gid_ref)

    d = D_MODEL
    merged = (gates_ref[:, 0:d].astype(F32) * _dot(ya, wa_ref[...])
              + gates_ref[:, d:2 * d].astype(F32) * _dot(yb, wb_ref[...])
              + gates_ref[:, 2 * d:3 * d].astype(F32) * _dot(yc, wc_ref[...]))
    x1 = x_ref[...] + _dot(merged.astype(BF16), wo_ref[...])
    o_ref[...] = x1
    x1_buf[...] = x1


def _mixer(l, x2d, a, u, v, q, k, vv, gates, c, seq):
    t = x2d.shape[0]
    r = ROWS_MIX
    nb = t // r
    prev = lambda w, n: pl.BlockSpec((r, w), lambda i: (jnp.maximum(jnp.minimum(i, nb - 1) - n, 0), 0))
    row = lambda w: prev(w, 0)
    routed = lambda shape: pl.BlockSpec(shape, lambda i: (jnp.maximum(i - 1, 0),) + (0,) * (len(shape) - 1))
    layer = [c["bd"], c["ps"], c["swc"], c["sbm"], c["bias"], c["wa"], c["wb"], c["wc"], c["wo"],
             c["gf"], c["wr"], c["br"]]
    in_specs = [row(D_MODEL), row(POOL_WIDTH), prev(POOL_WIDTH, 1), row(SGU_WIDTH), row(SGU_WIDTH),
                row(ATT_WIDTH),
                prev(ATT_WIDTH, 2), prev(ATT_WIDTH, 1), row(ATT_WIDTH),
                prev(ATT_WIDTH, 2), prev(ATT_WIDTH, 1), row(ATT_WIDTH),
                row(N_BRANCH * D_MODEL),
                _const_spec(c["invf"].shape), _const_spec(c["invr"].shape)] + [_layer_spec(w, l) for w in layer]
    return pl.pallas_call(
        functools.partial(_mixer_kernel, seq // r),
        grid=(nb + 1,),
        in_specs=in_specs,
        out_specs=[row(D_MODEL), routed((r * SUBLANES, LANES)), routed((1, 1, r))],
        out_shape=[jax.ShapeDtypeStruct((t, D_MODEL), F32), jax.ShapeDtypeStruct((t * SUBLANES, LANES), U32),
                   jax.ShapeDtypeStruct((nb, 1, r), I32)],
        scratch_shapes=[pltpu.VMEM((r + 2 * POOL_HALO, POOL_WIDTH), F32),
                        pltpu.VMEM((r, D_MODEL), F32)],
        compiler_params=_params(1),
        name="mixer",
    )(x2d, a, a, u, v, q, k, k, k, vv, vv, vv, gates, c["invf"], c["invr"], *layer)


ROUTER_ROWS = SUBLANES + N_GROUPS * EXPERTS_PER_GROUP


def _route(x, g_ref, wr_ref, br_ref, hp_ref, gid_ref):
    tm = x.shape[0]
    h = _rms(x, g_ref[...])
    h_hi = h.astype(BF16)
    h_lo = (h - h_hi.astype(F32)).astype(BF16)
    nr = ROUTER_ROWS
    l1 = _dot_nt(wr_ref[...], h_hi)
    l2 = _dot_nt(wr_ref[0:nr, :], h_lo)
    logits = l1[0:nr] + l1[nr:2 * nr] + l2 + br_ref[...]

    rows = lax.broadcasted_iota(I32, (SUBLANES, tm), 0).astype(F32)
    big = np.float32(1 << 20)
    gl = jnp.where(rows < N_GROUPS, logits[0:SUBLANES], NEG)
    gmax = jnp.max(gl, axis=0, keepdims=True)
    g_p = 1.0 / jnp.sum(jnp.exp(gl - gmax), axis=0, keepdims=True)
    g_idx = jnp.min(jnp.where(gl == gmax, rows, big), axis=0, keepdims=True)

    el = logits[SUBLANES:2 * SUBLANES]
    for gi in range(1, N_GROUPS):
        el = jnp.where(g_idx == gi, logits[SUBLANES * (gi + 1):SUBLANES * (gi + 2)], el)
    emax = jnp.max(el, axis=0, keepdims=True)
    esum = jnp.sum(jnp.exp(el - emax), axis=0, keepdims=True)
    i0 = jnp.min(jnp.where(el == emax, rows, big), axis=0, keepdims=True)
    el2 = jnp.where(rows == i0, -jnp.inf, el)
    m2 = jnp.max(el2, axis=0, keepdims=True)
    i1 = jnp.min(jnp.where(el2 == m2, rows, big), axis=0, keepdims=True)
    p0 = 1.0 / esum
    p1 = jnp.exp(m2 - emax) / esum
    w0 = g_p * p0 / (p0 + p1)
    w1 = g_p * p1 / (p0 + p1)
    gates = jnp.where(rows == i0, w0, 0.0) + jnp.where(rows == i1, w1, 0.0)

    gates_t = jnp.concatenate([gates, jnp.zeros((LANES - SUBLANES, tm), F32)], axis=0).T
    packed = _pack_pairs(h)
    for j in range(PACK_ROWS):
        hp_ref[_slab_rows(j, tm), :] = packed[:, j * LANES:(j + 1) * LANES]
    hp_ref[_slab_rows(PACK_ROWS, tm), :] = pltpu.bitcast(gates_t, U32)
    for j in range(PACK_ROWS + 1, SUBLANES):
        hp_ref[_slab_rows(j, tm), :] = jnp.zeros((tm, LANES), U32)
    gid_ref[...] = g_idx.astype(I32).reshape(1, 1, tm)


def _moe_kernel(padded, spad_ref, poffs_ref, ntiles_ref, hp_ref, win_ref, wout_ref, o_ref, xt_ref, yt_ref):
    c = pl.program_id(0)
    g = pl.program_id(1)
    seg0 = c * padded + poffs_ref[c * N_GROUPS + g]
    tr = MOE_TILE
    st = MOE_STRIDE

    def tile_body(t, carry):
        r0 = seg0 + t * tr

        def slab(entry):
            return pl.ds(pl.multiple_of(spad_ref[entry], SUBLANES), SUBLANES)

        def gather(i, cr):
            rr, entry = i * MOE_UNROLL, r0 + i * MOE_UNROLL
            for uu in range(MOE_UNROLL):
                xt_ref[pl.ds(rr + uu, SUBLANES, stride=st), :] = hp_ref[slab(entry + uu), :]
            return cr

        lax.fori_loop(0, tr // MOE_UNROLL, gather, 0)

        packed = jnp.concatenate([xt_ref[pl.ds(j * st, tr), :] for j in range(PACK_ROWS)], axis=1)
        xb = _unpack_pairs(packed).astype(BF16)
        gates = pltpu.bitcast(xt_ref[pl.ds(PACK_ROWS * st, tr), :], F32)
        acts = []
        for e in range(EXPERTS_PER_GROUP):
            hu = _dot(xb, win_ref[e])
            hg = hu[:, 0:D_EXPERT]
            acts.append((hg * _sigmoid(hg) * hu[:, D_EXPERT:] * gates[:, e:e + 1]).astype(BF16))
        y = _dot(jnp.concatenate(acts, axis=1), wout_ref[...].reshape(GROUP_FF, D_MODEL))
        for j in range(SUBLANES):
            yt_ref[pl.ds(j * st, tr), :] = y[:, j * LANES:(j + 1) * LANES]

        def scatter(i, cr):
            rr, entry = i * MOE_UNROLL, r0 + i * MOE_UNROLL
            for uu in range(MOE_UNROLL):
                o_ref[slab(entry + uu), :] = yt_ref[pl.ds(rr + uu, SUBLANES, stride=st), :]
            return cr

        lax.fori_loop(0, tr // MOE_UNROLL, scatter, 0)
        return carry

    lax.fori_loop(0, ntiles_ref[c * N_GROUPS + g], tile_body, 0)


def _moe(l, hp, spad, poffs, ntiles, c, chunk):
    t = hp.shape[0] // SUBLANES
    win, wout = c["win"], c["wout"]
    grid_spec = pltpu.PrefetchScalarGridSpec(
        num_scalar_prefetch=3,
        grid=(t // chunk, N_GROUPS),
        in_specs=[pl.BlockSpec((chunk * SUBLANES, LANES), lambda ci, g, *_: (ci, 0)),
                  pl.BlockSpec((None, EXPERTS_PER_GROUP) + win.shape[2:], lambda ci, g, *_: (l, g, 0, 0)),
                  pl.BlockSpec((None, EXPERTS_PER_GROUP) + wout.shape[2:], lambda ci, g, *_: (l, g, 0, 0))],
        out_specs=pl.BlockSpec((chunk * SUBLANES, LANES), lambda ci, g, *_: (ci, 0)),
        scratch_shapes=[pltpu.VMEM((MOE_STRIDE * SUBLANES, LANES), U32),
                        pltpu.VMEM((MOE_STRIDE * SUBLANES, LANES), F32)],
    )
    return pl.pallas_call(
        functools.partial(_moe_kernel, spad.shape[0] // (t // chunk)),
        grid_spec=grid_spec,
        out_shape=jax.ShapeDtypeStruct((t * SUBLANES, LANES), F32),
        compiler_params=_params(2),
        name="moe",
    )(spad, poffs, ntiles, hp, win, wout)


def _ple_kernel(x_ref, mo_ref, p_ref, g_ref, wpi_ref, wg_ref, o_ref):
    tm = x_ref.shape[0]
    mo = jnp.concatenate([mo_ref[_slab_rows(j, tm), :] for j in range(SUBLANES)], axis=1)
    x = x_ref[...] + mo
    hn = _rms(x, g_ref[...]).astype(BF16)
    gate = _sigmoid(_dot(hn, wg_ref[...]))
    emb = _dot(p_ref[...].astype(BF16), wpi_ref[...])
    o_ref[...] = x + emb * gate


def _ple(l, x2d, mo, p3d, c):
    t = x2d.shape[0]
    tm = ROWS_PROJ
    row = lambda w: pl.BlockSpec((tm, w), lambda i: (i, 0))
    layer = [c["g"], c["wpi"], c["wg"]]
    return pl.pallas_call(
        _ple_kernel,
        grid=(t // tm,),
        in_specs=[row(D_MODEL), pl.BlockSpec((tm * SUBLANES, LANES), lambda i: (i, 0)),
                  pl.BlockSpec((None, tm, PLE_DIM), lambda i: (l, i, 0))] + [_layer_spec(w, l) for w in layer],
        out_specs=row(D_MODEL),
        out_shape=jax.ShapeDtypeStruct((t, D_MODEL), F32),
        compiler_params=_params(1),
        name="ple",
    )(x2d, mo, p3d, *layer)


def _attention_bias(rel_bias):
    r = ROWS_MIX
    nk = (HALO_BLOCKS + 1) * r
    qi = np.arange(r)[:, None]
    kj = np.arange(nk)[None, :]
    qc = qi // CHUNK
    kc = kj // CHUNK
    valid = (kc >= qc) & (kc <= qc + LEFT_CHUNKS)
    p = nk + r
    off = np.arange(p)
    off = np.where(off < nk, off, off - p)
    rel = np.clip(HALO_BLOCKS * r - off, -REL_CLIP, REL_CLIP) + REL_CLIP
    per_off = rel_bias[..., rel].astype(F32)
    lead = per_off.shape[:-1]
    band = jnp.tile(per_off, (1, 1, r))[..., :r * (p - 1)].reshape(lead + (r, p - 1))[..., :nk]
    return jnp.where(jnp.asarray(valid), band * np.float32(LOG2E), NEG)


def _pool_inverse_counts():
    r = ROWS_MIX
    win = np.repeat(np.asarray(POOL_WINDOWS, np.float32), POOL_GROUP_DIM)[None, :]
    pos = np.arange(1, r + 1, dtype=np.float32)[:, None]
    return jnp.asarray(1.0 / np.minimum(pos, win), F32), jnp.asarray(1.0 / win, F32)


def _block_diag(w):
    nl, g, c, d = w.shape
    eye = jnp.eye(g, dtype=w.dtype)
    return jnp.einsum("gh,lgcd->lgchd", eye, w).reshape(nl, g * c, g * d)


def _prepare(prm):
    nl = prm["w_in"].shape[0]
    row = lambda v: v.reshape(nl, 1, -1).astype(F32)
    hm = _block_diag(jnp.full((1, MXU_DIM // HEAD_DIM, HEAD_DIM, HEAD_DIM), 1.0 / HEAD_DIM, F32))[0].astype(BF16)
    inproj = dict(
        g=row(prm["mix_norm_g"]), w_in=prm["w_in"].astype(BF16), sgn=row(prm["sgu_norm_g"]),
        gq=row(jnp.tile(prm["q_norm_g"], (1, ATT_HEADS))), gk=row(jnp.tile(prm["k_norm_g"], (1, ATT_HEADS))),
        hm=hm, gb=row(prm["gate_b"]))

    i = np.arange(SGU_BLOCK)
    tri = (i[None, :] // CHUNK) <= (i[:, None] // CHUNK)
    sw = jnp.where(jnp.asarray(tri), prm["sgu_w"], 0.0)
    swc = jnp.concatenate([sw[:, g] for g in range(SGU_GROUPS)], axis=2).astype(BF16)
    sbm = jnp.repeat(jnp.swapaxes(prm["sgu_b"], 1, 2), SGU_GROUP_DIM, axis=2).astype(F32)
    invf, invr = _pool_inverse_counts()
    mixer = dict(invf=invf, invr=invr, bd=_block_diag(prm["pool_w"]).astype(BF16), ps=row(prm["pool_scale"]),
                 swc=swc, sbm=sbm, bias=_attention_bias(prm["rel_bias"]),
                 wa=prm["w_branch_a"].astype(BF16), wb=prm["w_branch_b"].astype(BF16),
                 wc=prm["w_branch_c"].astype(BF16), wo=prm["w_out"].astype(BF16))

    pad = jnp.zeros((nl, SUBLANES - N_GROUPS, D_MODEL), F32)
    wr = jnp.concatenate([jnp.swapaxes(prm["w_group_router"], 1, 2), pad,
                          jnp.swapaxes(prm["w_expert_router"], 1, 2)], axis=1)
    wr_hi = wr.astype(BF16)
    wr_lo = (wr - wr_hi.astype(F32)).astype(BF16)
    br = jnp.concatenate([prm["b_group_router"], jnp.zeros((nl, SUBLANES - N_GROUPS), F32),
                          prm["b_expert_router"]], axis=1).reshape(nl, -1, 1).astype(F32)
    mixer.update(gf=row(prm["ffn_norm_g"]), wr=jnp.concatenate([wr_hi, wr_lo], axis=1), br=br)

    moe = dict(win=prm["w_expert_in"].astype(BF16),
               wout=prm["w_expert_out"].astype(BF16))

    ple = dict(g=row(prm["ple_norm_g"]), wpi=prm["w_ple_in"].astype(BF16), wg=prm["w_ple_gate"].astype(BF16))
    return inproj, mixer, moe, ple


def _sort_by_group(gid, chunk):
    tr = MOE_TILE
    gid = gid.reshape(-1, chunk)
    order = jnp.argsort(gid, axis=1, stable=True).astype(I32)
    groups = jnp.arange(N_GROUPS, dtype=I32)
    cnts = jnp.sum(gid[:, :, None] == groups, axis=1).astype(I32)
    offs = jnp.cumsum(cnts, axis=1) - cnts
    ntiles = (cnts + (tr - 1)) // tr
    poffs = (jnp.cumsum(ntiles, axis=1) - ntiles) * tr
    slot = jnp.arange(chunk + N_GROUPS * tr, dtype=I32)[None, :, None]
    seg = jnp.sum(slot >= poffs[:, None, 1:], axis=2)
    pick = lambda v: jnp.sum(jnp.where(seg[:, :, None] == groups, v[:, None, :], 0), axis=2)
    src = pick(offs) + jnp.clip(slot[:, :, 0] - pick(poffs), 0, jnp.maximum(pick(cnts) - 1, 0))
    spad = jnp.take_along_axis(order, jnp.minimum(src, chunk - 1), axis=1) * SUBLANES
    return spad.reshape(-1), poffs.reshape(-1).astype(I32), ntiles.reshape(-1).astype(I32)


def kernel(x, p, mix_norm_g, w_in, pool_w, pool_scale, sgu_norm_g, sgu_w, sgu_b, q_norm_g, k_norm_g, rel_bias, gate_b, w_branch_a, w_branch_b, w_branch_c, w_out, ffn_norm_g, w_group_router, b_group_router, w_expert_router, b_expert_router, w_expert_in, w_expert_out, ple_norm_g, w_ple_in, w_ple_gate):
    prm = dict(mix_norm_g=mix_norm_g, w_in=w_in, pool_w=pool_w, pool_scale=pool_scale, sgu_norm_g=sgu_norm_g,
               sgu_w=sgu_w, sgu_b=sgu_b, q_norm_g=q_norm_g, k_norm_g=k_norm_g, rel_bias=rel_bias, gate_b=gate_b,
               w_branch_a=w_branch_a, w_branch_b=w_branch_b, w_branch_c=w_branch_c, w_out=w_out,
               ffn_norm_g=ffn_norm_g, w_group_router=w_group_router, b_group_router=b_group_router,
               w_expert_router=w_expert_router, b_expert_router=b_expert_router, w_expert_in=w_expert_in,
               w_expert_out=w_expert_out, ple_norm_g=ple_norm_g, w_ple_in=w_ple_in, w_ple_gate=w_ple_gate)
    b, s, d = x.shape
    t = b * s
    depth = p.shape[0]
    chunk = min(MOE_CHUNK, t)
    assert d == D_MODEL and s % ROWS_MIX == 0 and t % ROWS_PROJ == 0 and t % ROWS_INPROJ == 0 and t % chunk == 0
    assert s // ROWS_MIX >= HALO_BLOCKS + 1
    c_in, c_mix, c_moe, c_ple = _prepare(prm)
    p3d = p.reshape(depth, t, PLE_DIM)
    xf = x.reshape(t, d)
    for l in range(depth):
        a, u, v, q, k, vv, gates = _in_proj(l, xf, c_in)
        xf, hp, gid = _mixer(l, xf, a, u, v, q, k, vv, gates, c_mix, s)
        spad, poffs, ntiles = _sort_by_group(gid, chunk)
        mo = _moe(l, hp, spad, poffs, ntiles, c_moe, chunk)
        xf = _ple(l, xf, mo, p3d, c_ple)
    return xf.reshape(b, s, d)
```

```python
import functools

import jax
import jax.numpy as jnp
import numpy as np
from jax import lax
from jax.experimental import pallas as pl
from jax.experimental.pallas import tpu as pltpu

F32 = jnp.float32
BF16 = jnp.bfloat16
U32 = jnp.uint32
I32 = jnp.int32

D_MODEL = 1024
CHUNK = 64
PLE_DIM = 256
EPS = 1e-6
POOL_WINDOWS = (2, 4, 8, 16)
POOL_WIDTH = 256
POOL_GROUP_DIM = 64
SGU_BLOCK = 128
SGU_WIDTH = 256
SGU_GROUPS = 4
SGU_GROUP_DIM = 64
HEAD_DIM = 64
ATT_WIDTH = 512
ATT_HEADS = 8
LEFT_CHUNKS = 8
REL_CLIP = 128
N_BRANCH = 3
N_GROUPS = 4
EXPERTS_PER_GROUP = 8
D_EXPERT = 128
GROUP_FF = EXPERTS_PER_GROUP * D_EXPERT

C_A, C_U, C_V, C_Q, C_K, C_VV, C_G = 0, 256, 512, 768, 1280, 1792, 2304

LANES = 128
SUBLANES = 8
MXU_DIM = 256
VMEM_LIMIT = 56 * 1024 * 1024

ROWS_INPROJ = 512
ROWS_PROJ = 512
ROWS_MIX = 512
ROWS_ATT = 256
HALO_BLOCKS = 2
POOL_HALO = 16
MOE_CHUNK = 2048
MOE_TILE = 256
MOE_STRIDE = MOE_TILE + 1
MOE_UNROLL = 32
PACK_W = D_MODEL // 2
PACK_ROWS = PACK_W // LANES
NEG = -1e30
LOG2E = float(np.log2(np.e))


def _const_spec(shape):
    nd = len(shape)
    return pl.BlockSpec(shape, lambda *_: (0,) * nd, pipeline_mode=pl.Buffered(1))


def _layer_spec(arr, l):
    nd = arr.ndim - 1
    return pl.BlockSpec((None,) + arr.shape[1:], lambda *_: (l,) + (0,) * nd, pipeline_mode=pl.Buffered(1))


def _params(n_axes):
    return pltpu.CompilerParams(dimension_semantics=("arbitrary",) * n_axes,
                                vmem_limit_bytes=VMEM_LIMIT)


def _rms(x, g):
    ms = jnp.mean(x * x, axis=-1, keepdims=True)
    return x * lax.rsqrt(ms + EPS) * g


def _gelu_tanh(x):
    c = np.float32(np.sqrt(2.0 / np.pi))
    return 0.5 * x * (1.0 + jnp.tanh(c * (x + np.float32(0.044715) * (x * x * x))))


def _sigmoid(x):
    return 0.5 * jnp.tanh(0.5 * x) + 0.5


def _dot(a, b):
    return jnp.dot(a, b, preferred_element_type=F32)


def _dot_nt(a, b):
    return lax.dot_general(a, b, (((1,), (1,)), ((), ())), preferred_element_type=F32)


def _pack_pairs(x):
    w = x.shape[1] // 2
    bits = pltpu.bitcast(x.astype(BF16).astype(F32), U32)
    return (bits[:, :w] >> 16) | (bits[:, w:] & np.uint32(0xFFFF0000))


def _unpack_pairs(p):
    lo = pltpu.bitcast(p << 16, F32)
    hi = pltpu.bitcast(p & np.uint32(0xFFFF0000), F32)
    return jnp.concatenate([lo, hi], axis=1)


def _slab_rows(j, n):
    return pl.ds(j, n, stride=SUBLANES)


def _inproj_kernel(x_ref, g_ref, w_ref, sgn_ref, gq_ref, gk_ref, hm_ref, gb_ref,
                   a_ref, u_ref, v_ref, q_ref, k_ref, vv_ref, gates_ref):
    h = _rms(x_ref[...], g_ref[...]).astype(BF16)

    def seg(c0, c1):
        return _dot(h, w_ref[:, c0:c1])

    def head_norm(z, g):
        zz = (z * z).astype(BF16)
        w = hm_ref.shape[0]
        ms = jnp.concatenate([_dot(zz[:, c0:c0 + w], hm_ref[...]) for c0 in range(0, ATT_WIDTH, w)], axis=1)
        return z * lax.rsqrt(ms + EPS) * g

    a_ref[...] = seg(C_A, C_U).astype(BF16)
    u_ref[...] = _gelu_tanh(seg(C_U, C_V)).astype(BF16)
    v_ref[...] = _rms(_gelu_tanh(seg(C_V, C_Q)), sgn_ref[...]).astype(BF16)
    q_ref[...] = (head_norm(seg(C_Q, C_K), gq_ref[...]) * np.float32(HEAD_DIM ** -0.5 * LOG2E)).astype(BF16)
    k_ref[...] = head_norm(seg(C_K, C_VV), gk_ref[...]).astype(BF16)
    vv_ref[...] = seg(C_VV, C_G).astype(BF16)
    step = 512
    for c0 in range(0, N_BRANCH * D_MODEL, step):
        z = seg(C_G + c0, C_G + c0 + step) + gb_ref[:, c0:c0 + step]
        gates_ref[:, c0:c0 + step] = _sigmoid(z).astype(BF16)


def _in_proj(l, x2d, c):
    t = x2d.shape[0]
    tm = ROWS_INPROJ
    row = lambda w: pl.BlockSpec((tm, w), lambda i: (i, 0))
    widths = (POOL_WIDTH, SGU_WIDTH, SGU_WIDTH, ATT_WIDTH, ATT_WIDTH, ATT_WIDTH, N_BRANCH * D_MODEL)
    layer = [c["g"], c["w_in"], c["sgn"], c["gq"], c["gk"]]
    return pl.pallas_call(
        _inproj_kernel,
        grid=(t // tm,),
        in_specs=[row(D_MODEL)] + [_layer_spec(w, l) for w in layer]
                 + [_const_spec(c["hm"].shape), _layer_spec(c["gb"], l)],
        out_specs=[row(w) for w in widths],
        out_shape=[jax.ShapeDtypeStruct((t, w), BF16) for w in widths],
        compiler_params=_params(1),
        name="in_proj",
    )(x2d, *layer, c["hm"], c["gb"])


def _mixer_kernel(blocks_per_seq,
                  x_ref, a_ref, ap_ref, u_ref, v_ref, q_ref,
                  kp_ref, kc_ref, vp_ref, vc_ref, gates_ref,
                  invf_ref, invr_ref, bd_ref, ps_ref, swc_ref, sb_ref, bias_ref,
                  wa_ref, wb_ref, wc_ref, wo_ref, gf_ref, wr_ref, br_ref,
                  o_ref, hp_ref, gid_ref, pool_buf, x1_buf):
    r = ROWS_MIX
    step = pl.program_id(0)
    blk = jnp.minimum(step, pl.num_programs(0) - 2)
    bi = blk % blocks_per_seq
    first = bi == 0

    @pl.when(step == 0)
    def _():
        x1_buf[...] = jnp.zeros_like(x1_buf)

    a = a_ref[...].astype(F32)
    halo = jnp.where(first, 0.0, ap_ref[r - POOL_HALO:, :].astype(F32))
    pool_buf[0:POOL_HALO, :] = jnp.zeros((POOL_HALO, POOL_WIDTH), F32)
    pool_buf[POOL_HALO:2 * POOL_HALO, :] = halo
    pool_buf[2 * POOL_HALO:, :] = a
    lane_a = lax.broadcasted_iota(I32, (r, POOL_WIDTH), 1) // POOL_GROUP_DIM
    wsum = None
    for gi, k in enumerate((1, 2, 4, 8)):
        cur = pool_buf[POOL_HALO:, :] + pool_buf[POOL_HALO - k:r + 2 * POOL_HALO - k, :]
        pool_buf[POOL_HALO:, :] = cur
        level = cur[POOL_HALO:, :]
        wsum = level if wsum is None else jnp.where(lane_a >= gi, level, wsum)
    inv = jnp.where(first, invf_ref[...], invr_ref[...])
    pooled = (wsum * inv - a).astype(BF16)
    ya = (_dot(pooled, bd_ref[...]) * ps_ref[...]).astype(BF16)

    vb = v_ref[...]
    lane_b = lax.broadcasted_iota(I32, (SGU_BLOCK, SGU_WIDTH), 1) // SGU_GROUP_DIM
    zero_b = jnp.zeros((SGU_BLOCK, SGU_WIDTH), BF16)
    mixed = []
    for s in range(r // SGU_BLOCK):
        blk = vb[s * SGU_BLOCK:(s + 1) * SGU_BLOCK, :]
        rhs = jnp.concatenate([jnp.where(lane_b == g, blk, zero_b) for g in range(SGU_GROUPS)], axis=0)
        mixed.append(_dot(swc_ref[...], rhs) + sb_ref[...])
    yb = (u_ref[...].astype(F32) * jnp.concatenate(mixed, axis=0)).astype(BF16)

    ra = ROWS_ATT
    nsub = r // ra
    nblk = HALO_BLOCKS + 1
    lane_c = lax.broadcasted_iota(I32, (ra, LANES), 1)
    zero_q = jnp.zeros((ra, LANES), BF16)
    yc = []
    for sub in range(nsub):
        lo = (nsub + sub - HALO_BLOCKS) * ra
        sub_idx = bi * nsub + sub
        blk_ok = [sub_idx >= HALO_BLOCKS - j for j in range(HALO_BLOCKS)] + [None]
        ycs = []
        for pr in range(ATT_HEADS // 2):
            cols = slice(pr * LANES, (pr + 1) * LANES)
            qp = q_ref[sub * ra:(sub + 1) * ra, cols]
            qs = jnp.concatenate([jnp.where(lane_c < HEAD_DIM, qp, zero_q),
                                  jnp.where(lane_c >= HEAD_DIM, qp, zero_q)], axis=0)
            kcat = jnp.concatenate([kp_ref[lo:, cols], kc_ref[0:(sub + 1) * ra, cols]], axis=0)
            vcat = jnp.concatenate([vp_ref[lo:, cols], vc_ref[0:(sub + 1) * ra, cols]], axis=0)
            bias = []
            for j in range(nblk):
                bj = bias_ref[2 * pr:2 * pr + 2, :, j * ra:(j + 1) * ra].reshape(2 * ra, ra)
                bias.append(bj if blk_ok[j] is None else jnp.where(blk_ok[j], bj, NEG))
            sc = _dot_nt(qs, kcat) + jnp.concatenate(bias, axis=1)
            m = jnp.max(sc, axis=-1, keepdims=True)
            p = jnp.exp2(sc - m)
            denom = jnp.sum(p, axis=-1, keepdims=True)
            o = _dot(p.astype(BF16), vcat) / denom
            ycs.append(jnp.where(lane_c < HEAD_DIM, o[0:ra], o[ra:2 * ra]))
        yc.append(jnp.concatenate(ycs, axis=1))
    yc = jnp.concatenate(yc, axis=0).astype(BF16)

    _route(x1_buf[...], gf_ref, wr_ref, br_ref, hp_ref, gid_ref)

    d = D_MODEL
    merged = (gates_ref[:, 0:d].astype(F32) * _dot(ya, wa_ref[...])
              + gates_ref[:, d:2 * d].astype(F32) * _dot(yb, wb_ref[...])
              + gates_ref[:, 2 * d:3 * d].astype(F32) * _dot(yc, wc_ref[...]))
    x1 = x_ref[...] + _dot(merged.astype(BF16), wo_ref[...])
    o_ref[...] = x1
    x1_buf[...] = x1


def _mixer(l, x2d, a, u, v, q, k, vv, gates, c, seq):
    t = x2d.shape[0]
    r = ROWS_MIX
    nb = t // r
    prev = lambda w, n: pl.BlockSpec((r, w), lambda i: (jnp.maximum(jnp.minimum(i, nb - 1) - n, 0), 0))
    row = lambda w: prev(w, 0)
    routed = lambda shape: pl.BlockSpec(shape, lambda i: (jnp.maximum(i - 1, 0),) + (0,) * (len(shape) - 1))
    layer = [c["bd"], c["ps"], c["swc"], c["sbm"], c["bias"], c["wa"], c["wb"], c["wc"], c["wo"],
             c["gf"], c["wr"], c["br"]]
    in_specs = [row(D_MODEL), row(POOL_WIDTH), prev(POOL_WIDTH, 1), row(SGU_WIDTH), row(SGU_WIDTH),
                row(ATT_WIDTH),
                prev(ATT_WIDTH, 1), row(ATT_WIDTH),
                prev(ATT_WIDTH, 1), row(ATT_WIDTH),
                row(N_BRANCH * D_MODEL),
                _const_spec(c["invf"].shape), _const_spec(c["invr"].shape)] + [_layer_spec(w, l) for w in layer]
    return pl.pallas_call(
        functools.partial(_mixer_kernel, seq // r),
        grid=(nb + 1,),
        in_specs=in_specs,
        out_specs=[row(D_MODEL), routed((r * SUBLANES, LANES)), routed((1, 1, r))],
        out_shape=[jax.ShapeDtypeStruct((t, D_MODEL), F32), jax.ShapeDtypeStruct((t * SUBLANES, LANES), U32),
                   jax.ShapeDtypeStruct((nb, 1, r), I32)],
        scratch_shapes=[pltpu.VMEM((r + 2 * POOL_HALO, POOL_WIDTH), F32),
                        pltpu.VMEM((r, D_MODEL), F32)],
        compiler_params=_params(1),
        name="mixer",
    )(x2d, a, a, u, v, q, k, k, vv, vv, gates, c["invf"], c["invr"], *layer)


ROUTER_ROWS = SUBLANES + N_GROUPS * EXPERTS_PER_GROUP


def _route(x, g_ref, wr_ref, br_ref, hp_ref, gid_ref):
    tm = x.shape[0]
    h = _rms(x, g_ref[...])
    h_hi = h.astype(BF16)
    h_lo = (h - h_hi.astype(F32)).astype(BF16)
    nr = ROUTER_ROWS
    l1 = _dot_nt(wr_ref[...], h_hi)
    l2 = _dot_nt(wr_ref[0:nr, :], h_lo)
    logits = l1[0:nr] + l1[nr:2 * nr] + l2 + br_ref[...]

    rows = lax.broadcasted_iota(I32, (SUBLANES, tm), 0).astype(F32)
    big = np.float32(1 << 20)
    gl = jnp.where(rows < N_GROUPS, logits[0:SUBLANES], NEG)
    gmax = jnp.max(gl, axis=0, keepdims=True)
    g_p = 1.0 / jnp.sum(jnp.exp(gl - gmax), axis=0, keepdims=True)
    g_idx = jnp.min(jnp.where(gl == gmax, rows, big), axis=0, keepdims=True)

    el = logits[SUBLANES:2 * SUBLANES]
    for gi in range(1, N_GROUPS):
        el = jnp.where(g_idx == gi, logits[SUBLANES * (gi + 1):SUBLANES * (gi + 2)], el)
    emax = jnp.max(el, axis=0, keepdims=True)
    esum = jnp.sum(jnp.exp(el - emax), axis=0, keepdims=True)
    i0 = jnp.min(jnp.where(el == emax, rows, big), axis=0, keepdims=True)
    el2 = jnp.where(rows == i0, -jnp.inf, el)
    m2 = jnp.max(el2, axis=0, keepdims=True)
    i1 = jnp.min(jnp.where(el2 == m2, rows, big), axis=0, keepdims=True)
    p0 = 1.0 / esum
    p1 = jnp.exp(m2 - emax) / esum
    w0 = g_p * p0 / (p0 + p1)
    w1 = g_p * p1 / (p0 + p1)
    gates = jnp.where(rows == i0, w0, 0.0) + jnp.where(rows == i1, w1, 0.0)

    gates_t = jnp.concatenate([gates, jnp.zeros((LANES - SUBLANES, tm), F32)], axis=0).T
    packed = _pack_pairs(h)
    for j in range(PACK_ROWS):
        hp_ref[_slab_rows(j, tm), :] = packed[:, j * LANES:(j + 1) * LANES]
    hp_ref[_slab_rows(PACK_ROWS, tm), :] = pltpu.bitcast(gates_t, U32)
    for j in range(PACK_ROWS + 1, SUBLANES):
        hp_ref[_slab_rows(j, tm), :] = jnp.zeros((tm, LANES), U32)
    gid_ref[...] = g_idx.astype(I32).reshape(1, 1, tm)


def _moe_kernel(padded, spad_ref, poffs_ref, ntiles_ref, hp_ref, win_ref, wout_ref, o_ref, xt_ref, yt_ref):
    c = pl.program_id(0)
    g = pl.program_id(1)
    seg0 = c * padded + poffs_ref[c * N_GROUPS + g]
    tr = MOE_TILE
    st = MOE_STRIDE

    def tile_body(t, carry):
        r0 = seg0 + t * tr

        def slab(entry):
            return pl.ds(pl.multiple_of(spad_ref[entry], SUBLANES), SUBLANES)

        def gather(i, cr):
            rr, entry = i * MOE_UNROLL, r0 + i * MOE_UNROLL
            for uu in range(MOE_UNROLL):
                xt_ref[pl.ds(rr + uu, SUBLANES, stride=st), :] = hp_ref[slab(entry + uu), :]
            return cr

        lax.fori_loop(0, tr // MOE_UNROLL, gather, 0)

        packed = jnp.concatenate([xt_ref[pl.ds(j * st, tr), :] for j in range(PACK_ROWS)], axis=1)
        xb = _unpack_pairs(packed).astype(BF16)
        gates = pltpu.bitcast(xt_ref[pl.ds(PACK_ROWS * st, tr), :], F32)
        acts = []
        for e in range(EXPERTS_PER_GROUP):
            hu = _dot(xb, win_ref[e])
            hg = hu[:, 0:D_EXPERT]
            acts.append((hg * _sigmoid(hg) * hu[:, D_EXPERT:] * gates[:, e:e + 1]).astype(BF16))
        y = _dot(jnp.concatenate(acts, axis=1), wout_ref[...].reshape(GROUP_FF, D_MODEL))
        for j in range(SUBLANES):
            yt_ref[pl.ds(j * st, tr), :] = y[:, j * LANES:(j + 1) * LANES]

        def scatter(i, cr):
            rr, entry = i * MOE_UNROLL, r0 + i * MOE_UNROLL
            for uu in range(MOE_UNROLL):
                o_ref[slab(entry + uu), :] = yt_ref[pl.ds(rr + uu, SUBLANES, stride=st), :]
            return cr

        lax.fori_loop(0, tr // MOE_UNROLL, scatter, 0)
        return carry

    lax.fori_loop(0, ntiles_ref[c * N_GROUPS + g], tile_body, 0)


def _moe(l, hp, spad, poffs, ntiles, c, chunk):
    t = hp.shape[0] // SUBLANES
    win, wout = c["win"], c["wout"]
    grid_spec = pltpu.PrefetchScalarGridSpec(
        num_scalar_prefetch=3,
        grid=(t // chunk, N_GROUPS),
        in_specs=[pl.BlockSpec((chunk * SUBLANES, LANES), lambda ci, g, *_: (ci, 0)),
                  pl.BlockSpec((None, EXPERTS_PER_GROUP) + win.shape[2:], lambda ci, g, *_: (l, g, 0, 0)),
                  pl.BlockSpec((None, EXPERTS_PER_GROUP) + wout.shape[2:], lambda ci, g, *_: (l, g, 0, 0))],
        out_specs=pl.BlockSpec((chunk * SUBLANES, LANES), lambda ci, g, *_: (ci, 0)),
        scratch_shapes=[pltpu.VMEM((MOE_STRIDE * SUBLANES, LANES), U32),
                        pltpu.VMEM((MOE_STRIDE * SUBLANES, LANES), F32)],
    )
    return pl.pallas_call(
        functools.partial(_moe_kernel, spad.shape[0] // (t // chunk)),
        grid_spec=grid_spec,
        out_shape=jax.ShapeDtypeStruct((t * SUBLANES, LANES), F32),
        compiler_params=_params(2),
        name="moe",
    )(spad, poffs, ntiles, hp, win, wout)


def _ple_kernel(x_ref, mo_ref, p_ref, g_ref, wpi_ref, wg_ref, o_ref):
    tm = x_ref.shape[0]
    mo = jnp.concatenate([mo_ref[_slab_rows(j, tm), :] for j in range(SUBLANES)], axis=1)
    x = x_ref[...] + mo
    hn = _rms(x, g_ref[...]).astype(BF16)
    gate = _sigmoid(_dot(hn, wg_ref[...]))
    emb = _dot(p_ref[...].astype(BF16), wpi_ref[...])
    o_ref[...] = x + emb * gate


def _ple(l, x2d, mo, p3d, c):
    t = x2d.shape[0]
    tm = ROWS_PROJ
    row = lambda w: pl.BlockSpec((tm, w), lambda i: (i, 0))
    layer = [c["g"], c["wpi"], c["wg"]]
    return pl.pallas_call(
        _ple_kernel,
        grid=(t // tm,),
        in_specs=[row(D_MODEL), pl.BlockSpec((tm * SUBLANES, LANES), lambda i: (i, 0)),
                  pl.BlockSpec((None, tm, PLE_DIM), lambda i: (l, i, 0))] + [_layer_spec(w, l) for w in layer],
        out_specs=row(D_MODEL),
        out_shape=jax.ShapeDtypeStruct((t, D_MODEL), F32),
        compiler_params=_params(1),
        name="ple",
    )(x2d, mo, p3d, *layer)


def _attention_bias(rel_bias):
    r = ROWS_ATT
    nk = (HALO_BLOCKS + 1) * r
    qi = np.arange(r)[:, None]
    kj = np.arange(nk)[None, :]
    qc = qi // CHUNK
    kc = kj // CHUNK
    valid = (kc >= qc) & (kc <= qc + LEFT_CHUNKS)
    p = nk + r
    off = np.arange(p)
    off = np.where(off < nk, off, off - p)
    rel = np.clip(HALO_BLOCKS * r - off, -REL_CLIP, REL_CLIP) + REL_CLIP
    per_off = rel_bias[..., rel].astype(F32)
    lead = per_off.shape[:-1]
    band = jnp.tile(per_off, (1, 1, r))[..., :r * (p - 1)].reshape(lead + (r, p - 1))[..., :nk]
    return jnp.where(jnp.asarray(valid), band * np.float32(LOG2E), NEG)


def _pool_inverse_counts():
    r = ROWS_MIX
    win = np.repeat(np.asarray(POOL_WINDOWS, np.float32), POOL_GROUP_DIM)[None, :]
    pos = np.arange(1, r + 1, dtype=np.float32)[:, None]
    return jnp.asarray(1.0 / np.minimum(pos, win), F32), jnp.asarray(1.0 / win, F32)


def _block_diag(w):
    nl, g, c, d = w.shape
    eye = jnp.eye(g, dtype=w.dtype)
    return jnp.einsum("gh,lgcd->lgchd", eye, w).reshape(nl, g * c, g * d)


def _prepare(prm):
    nl = prm["w_in"].shape[0]
    row = lambda v: v.reshape(nl, 1, -1).astype(F32)
    hm = _block_diag(jnp.full((1, MXU_DIM // HEAD_DIM, HEAD_DIM, HEAD_DIM), 1.0 / HEAD_DIM, F32))[0].astype(BF16)
    inproj = dict(
        g=row(prm["mix_norm_g"]), w_in=prm["w_in"].astype(BF16), sgn=row(prm["sgu_norm_g"]),
        gq=row(jnp.tile(prm["q_norm_g"], (1, ATT_HEADS))), gk=row(jnp.tile(prm["k_norm_g"], (1, ATT_HEADS))),
        hm=hm, gb=row(prm["gate_b"]))

    i = np.arange(SGU_BLOCK)
    tri = (i[None, :] // CHUNK) <= (i[:, None] // CHUNK)
    sw = jnp.where(jnp.asarray(tri), prm["sgu_w"], 0.0)
    swc = jnp.concatenate([sw[:, g] for g in range(SGU_GROUPS)], axis=2).astype(BF16)
    sbm = jnp.repeat(jnp.swapaxes(prm["sgu_b"], 1, 2), SGU_GROUP_DIM, axis=2).astype(F32)
    invf, invr = _pool_inverse_counts()
    mixer = dict(invf=invf, invr=invr, bd=_block_diag(prm["pool_w"]).astype(BF16), ps=row(prm["pool_scale"]),
                 swc=swc, sbm=sbm, bias=_attention_bias(prm["rel_bias"]),
                 wa=prm["w_branch_a"].astype(BF16), wb=prm["w_branch_b"].astype(BF16),
                 wc=prm["w_branch_c"].astype(BF16), wo=prm["w_out"].astype(BF16))

    pad = jnp.zeros((nl, SUBLANES - N_GROUPS, D_MODEL), F32)
    wr = jnp.concatenate([jnp.swapaxes(prm["w_group_router"], 1, 2), pad,
                          jnp.swapaxes(prm["w_expert_router"], 1, 2)], axis=1)
    wr_hi = wr.astype(BF16)
    wr_lo = (wr - wr_hi.astype(F32)).astype(BF16)
    br = jnp.concatenate([prm["b_group_router"], jnp.zeros((nl, SUBLANES - N_GROUPS), F32),
                          prm["b_expert_router"]], axis=1).reshape(nl, -1, 1).astype(F32)
    mixer.update(gf=row(prm["ffn_norm_g"]), wr=jnp.concatenate([wr_hi, wr_lo], axis=1), br=br)

    moe = dict(win=prm["w_expert_in"].astype(BF16),
               wout=prm["w_expert_out"].astype(BF16))

    ple = dict(g=row(prm["ple_norm_g"]), wpi=prm["w_ple_in"].astype(BF16), wg=prm["w_ple_gate"].astype(BF16))
    return inproj, mixer, moe, ple


def _sort_by_group(gid, chunk):
    tr = MOE_TILE
    gid = gid.reshape(-1, chunk)
    order = jnp.argsort(gid, axis=1, stable=True).astype(I32)
    groups = jnp.arange(N_GROUPS, dtype=I32)
    cnts = jnp.sum(gid[:, :, None] == groups, axis=1).astype(I32)
    offs = jnp.cumsum(cnts, axis=1) - cnts
    ntiles = (cnts + (tr - 1)) // tr
    poffs = (jnp.cumsum(ntiles, axis=1) - ntiles) * tr
    slot = jnp.arange(chunk + N_GROUPS * tr, dtype=I32)[None, :, None]
    seg = jnp.sum(slot >= poffs[:, None, 1:], axis=2)
    pick = lambda v: jnp.sum(jnp.where(seg[:, :, None] == groups, v[:, None, :], 0), axis=2)
    src = pick(offs) + jnp.clip(slot[:, :, 0] - pick(poffs), 0, jnp.maximum(pick(cnts) - 1, 0))
    spad = jnp.take_along_axis(order, jnp.minimum(src, chunk - 1), axis=1) * SUBLANES
    return spad.reshape(-1), poffs.reshape(-1).astype(I32), ntiles.reshape(-1).astype(I32)


def kernel(x, p, mix_norm_g, w_in, pool_w, pool_scale, sgu_norm_g, sgu_w, sgu_b, q_norm_g, k_norm_g, rel_bias, gate_b, w_branch_a, w_branch_b, w_branch_c, w_out, ffn_norm_g, w_group_router, b_group_router, w_expert_router, b_expert_router, w_expert_in, w_expert_out, ple_norm_g, w_ple_in, w_ple_gate):
    prm = dict(mix_norm_g=mix_norm_g, w_in=w_in, pool_w=pool_w, pool_scale=pool_scale, sgu_norm_g=sgu_norm_g,
               sgu_w=sgu_w, sgu_b=sgu_b, q_norm_g=q_norm_g, k_norm_g=k_norm_g, rel_bias=rel_bias, gate_b=gate_b,
               w_branch_a=w_branch_a, w_branch_b=w_branch_b, w_branch_c=w_branch_c, w_out=w_out,
               ffn_norm_g=ffn_norm_g, w_group_router=w_group_router, b_group_router=b_group_router,
               w_expert_router=w_expert_router, b_expert_router=b_expert_router, w_expert_in=w_expert_in,
               w_expert_out=w_expert_out, ple_norm_g=ple_norm_g, w_ple_in=w_ple_in, w_ple_gate=w_ple_gate)
    b, s, d = x.shape
    t = b * s
    depth = p.shape[0]
    chunk = min(MOE_CHUNK, t)
    assert d == D_MODEL and s % ROWS_MIX == 0 and t % ROWS_PROJ == 0 and t % ROWS_INPROJ == 0 and t % chunk == 0
    assert ROWS_MIX % ROWS_ATT == 0 and ROWS_MIX // ROWS_ATT >= HALO_BLOCKS
    c_in, c_mix, c_moe, c_ple = _prepare(prm)
    p3d = p.reshape(depth, t, PLE_DIM)
    xf = x.reshape(t, d)
    for l in range(depth):
        a, u, v, q, k, vv, gates = _in_proj(l, xf, c_in)
        xf, hp, gid = _mixer(l, xf, a, u, v, q, k, vv, gates, c_mix, s)
        spad, poffs, ntiles = _sort_by_group(gid, chunk)
        mo = _moe(l, hp, spad, poffs, ntiles, c_moe, chunk)
        xf = _ple(l, xf, mo, p3d, c_ple)
    return xf.reshape(b, s, d)
```

```python
import functools

import jax
import jax.numpy as jnp
import numpy as np
from jax import lax
from jax.experimental import pallas as pl
from jax.experimental.pallas import tpu as pltpu

F32 = jnp.float32
BF16 = jnp.bfloat16
U32 = jnp.uint32
I32 = jnp.int32

D_MODEL = 1024
CHUNK = 64
PLE_DIM = 256
EPS = 1e-6
POOL_WINDOWS = (2, 4, 8, 16)
POOL_WIDTH = 256
POOL_GROUP_DIM = 64
SGU_BLOCK = 128
SGU_WIDTH = 256
SGU_GROUPS = 4
SGU_GROUP_DIM = 64
HEAD_DIM = 64
ATT_WIDTH = 512
ATT_HEADS = 8
LEFT_CHUNKS = 8
REL_CLIP = 128
N_BRANCH = 3
N_GROUPS = 4
EXPERTS_PER_GROUP = 8
D_EXPERT = 128
GROUP_FF = EXPERTS_PER_GROUP * D_EXPERT

C_A, C_U, C_V, C_Q, C_K, C_VV, C_G = 0, 256, 512, 768, 1280, 1792, 2304

LANES = 128
SUBLANES = 8
MXU_DIM = 256
VMEM_LIMIT = 56 * 1024 * 1024

ROWS_INPROJ = 512
ROWS_PROJ = 512
ROWS_MIX = 512
ROWS_ATT = 256
HALO_BLOCKS = 2
POOL_HALO = 16
MOE_CHUNK = 2048
MOE_TILE = 256
MOE_STRIDE = MOE_TILE + 1
MOE_UNROLL = 32
PACK_W = D_MODEL // 2
PACK_ROWS = PACK_W // LANES
NEG = -1e30
GATE_SEG = 512
N_GATE_SEGS = N_BRANCH * D_MODEL // GATE_SEG
LOG2E = float(np.log2(np.e))


def _const_spec(shape):
    nd = len(shape)
    return pl.BlockSpec(shape, lambda *_: (0,) * nd, pipeline_mode=pl.Buffered(1))


def _layer_spec(arr, l):
    nd = arr.ndim - 1
    return pl.BlockSpec((None,) + arr.shape[1:], lambda *_: (l,) + (0,) * nd, pipeline_mode=pl.Buffered(1))


def _params(n_axes):
    return pltpu.CompilerParams(dimension_semantics=("arbitrary",) * n_axes,
                                vmem_limit_bytes=VMEM_LIMIT)


def _rms(x, g):
    ms = jnp.mean(x * x, axis=-1, keepdims=True)
    return x * lax.rsqrt(ms + EPS) * g


def _gelu_tanh(x):
    c = np.float32(np.sqrt(2.0 / np.pi))
    return 0.5 * x * (1.0 + jnp.tanh(c * (x + np.float32(0.044715) * (x * x * x))))


def _sigmoid(x):
    return 0.5 * jnp.tanh(0.5 * x) + 0.5


def _dot(a, b):
    return jnp.dot(a, b, preferred_element_type=F32)


def _dot_nt(a, b):
    return lax.dot_general(a, b, (((1,), (1,)), ((), ())), preferred_element_type=F32)


def _pack_pairs(x):
    w = x.shape[1] // 2
    bits = pltpu.bitcast(x.astype(BF16).astype(F32), U32)
    return (bits[:, :w] >> 16) | (bits[:, w:] & np.uint32(0xFFFF0000))


def _unpack_pairs(p):
    lo = pltpu.bitcast(p << 16, F32)
    hi = pltpu.bitcast(p & np.uint32(0xFFFF0000), F32)
    return jnp.concatenate([lo, hi], axis=1)


def _slab_rows(j, n):
    return pl.ds(j, n, stride=SUBLANES)


def _inproj_parts(x_ref, g_ref, w_ref, sgn_ref, gq_ref, gk_ref, hm_ref, gb_ref):
    h = _rms(x_ref[...], g_ref[...]).astype(BF16)

    def seg(c0, c1):
        return _dot(h, w_ref[:, c0:c1])

    def head_norm(z, g):
        zz = (z * z).astype(BF16)
        w = hm_ref.shape[0]
        ms = jnp.concatenate([_dot(zz[:, c0:c0 + w], hm_ref[...]) for c0 in range(0, ATT_WIDTH, w)], axis=1)
        return z * lax.rsqrt(ms + EPS) * g

    def branches(a_ref, u_ref, v_ref, q_ref, k_ref, vv_ref):
        a_ref[...] = seg(C_A, C_U).astype(BF16)
        u_ref[...] = _gelu_tanh(seg(C_U, C_V)).astype(BF16)
        v_ref[...] = _rms(_gelu_tanh(seg(C_V, C_Q)), sgn_ref[...]).astype(BF16)
        q_ref[...] = (head_norm(seg(C_Q, C_K), gq_ref[...]) * np.float32(HEAD_DIM ** -0.5 * LOG2E)).astype(BF16)
        k_ref[...] = head_norm(seg(C_K, C_VV), gk_ref[...]).astype(BF16)
        vv_ref[...] = seg(C_VV, C_G).astype(BF16)

    def gate_segment(i, gates_ref):
        c0 = i * GATE_SEG
        z = seg(C_G + c0, C_G + c0 + GATE_SEG) + gb_ref[:, c0:c0 + GATE_SEG]
        gates_ref[:, c0:c0 + GATE_SEG] = _sigmoid(z).astype(BF16)

    return branches, gate_segment


def _inproj_kernel(x_ref, g_ref, w_ref, sgn_ref, gq_ref, gk_ref, hm_ref, gb_ref,
                   a_ref, u_ref, v_ref, q_ref, k_ref, vv_ref, gates_ref):
    branches, gate_segment = _inproj_parts(x_ref, g_ref, w_ref, sgn_ref, gq_ref, gk_ref, hm_ref, gb_ref)
    branches(a_ref, u_ref, v_ref, q_ref, k_ref, vv_ref)
    for i in range(N_GATE_SEGS):
        gate_segment(i, gates_ref)


def _in_proj(l, x2d, c):
    t = x2d.shape[0]
    tm = ROWS_INPROJ
    row = lambda w: pl.BlockSpec((tm, w), lambda i: (i, 0))
    widths = (POOL_WIDTH, SGU_WIDTH, SGU_WIDTH, ATT_WIDTH, ATT_WIDTH, ATT_WIDTH, N_BRANCH * D_MODEL)
    layer = [c["g"], c["w_in"], c["sgn"], c["gq"], c["gk"]]
    return pl.pallas_call(
        _inproj_kernel,
        grid=(t // tm,),
        in_specs=[row(D_MODEL)] + [_layer_spec(w, l) for w in layer]
                 + [_const_spec(c["hm"].shape), _layer_spec(c["gb"], l)],
        out_specs=[row(w) for w in widths],
        out_shape=[jax.ShapeDtypeStruct((t, w), BF16) for w in widths],
        compiler_params=_params(1),
        name="in_proj",
    )(x2d, *layer, c["hm"], c["gb"])


def _mixer_kernel(blocks_per_seq,
                  x_ref, a_ref, ap_ref, u_ref, v_ref, q_ref,
                  kp_ref, kc_ref, vp_ref, vc_ref, gates_ref,
                  invf_ref, invr_ref, bd_ref, ps_ref, swc_ref, sb_ref, bias_ref,
                  wa_ref, wb_ref, wc_ref, wo_ref, gf_ref, wr_ref, br_ref,
                  o_ref, hp_ref, gid_ref, pool_buf, x1_buf, between=None):
    r = ROWS_MIX
    step = pl.program_id(0)
    blk = jnp.minimum(step, pl.num_programs(0) - 2)
    bi = blk % blocks_per_seq
    first = bi == 0

    @pl.when(step == 0)
    def _():
        x1_buf[...] = jnp.zeros_like(x1_buf)

    a = a_ref[...].astype(F32)
    halo = jnp.where(first, 0.0, ap_ref[r - POOL_HALO:, :].astype(F32))
    pool_buf[0:POOL_HALO, :] = jnp.zeros((POOL_HALO, POOL_WIDTH), F32)
    pool_buf[POOL_HALO:2 * POOL_HALO, :] = halo
    pool_buf[2 * POOL_HALO:, :] = a
    lane_a = lax.broadcasted_iota(I32, (r, POOL_WIDTH), 1) // POOL_GROUP_DIM
    wsum = None
    for gi, k in enumerate((1, 2, 4, 8)):
        cur = pool_buf[POOL_HALO:, :] + pool_buf[POOL_HALO - k:r + 2 * POOL_HALO - k, :]
        pool_buf[POOL_HALO:, :] = cur
        level = cur[POOL_HALO:, :]
        wsum = level if wsum is None else jnp.where(lane_a >= gi, level, wsum)
    inv = jnp.where(first, invf_ref[...], invr_ref[...])
    pooled = (wsum * inv - a).astype(BF16)
    ya = (_dot(pooled, bd_ref[...]) * ps_ref[...]).astype(BF16)

    vb = v_ref[...]
    lane_b = lax.broadcasted_iota(I32, (SGU_BLOCK, SGU_WIDTH), 1) // SGU_GROUP_DIM
    zero_b = jnp.zeros((SGU_BLOCK, SGU_WIDTH), BF16)
    mixed = []
    for s in range(r // SGU_BLOCK):
        blk = vb[s * SGU_BLOCK:(s + 1) * SGU_BLOCK, :]
        rhs = jnp.concatenate([jnp.where(lane_b == g, blk, zero_b) for g in range(SGU_GROUPS)], axis=0)
        mixed.append(_dot(swc_ref[...], rhs) + sb_ref[...])
    yb = (u_ref[...].astype(F32) * jnp.concatenate(mixed, axis=0)).astype(BF16)

    ra = ROWS_ATT
    nsub = r // ra
    nblk = HALO_BLOCKS + 1
    lane_c = lax.broadcasted_iota(I32, (ra, LANES), 1)
    zero_q = jnp.zeros((ra, LANES), BF16)
    yc = []
    for sub in range(nsub):
        lo = (nsub + sub - HALO_BLOCKS) * ra
        sub_idx = bi * nsub + sub
        blk_ok = [sub_idx >= HALO_BLOCKS - j for j in range(HALO_BLOCKS)] + [None]
        ycs = []
        for pr in range(ATT_HEADS // 2):
            cols = slice(pr * LANES, (pr + 1) * LANES)
            qp = q_ref[sub * ra:(sub + 1) * ra, cols]
            qs = jnp.concatenate([jnp.where(lane_c < HEAD_DIM, qp, zero_q),
                                  jnp.where(lane_c >= HEAD_DIM, qp, zero_q)], axis=0)
            kcat = jnp.concatenate([kp_ref[lo:, cols], kc_ref[0:(sub + 1) * ra, cols]], axis=0)
            vcat = jnp.concatenate([vp_ref[lo:, cols], vc_ref[0:(sub + 1) * ra, cols]], axis=0)
            bias = []
            for j in range(nblk):
                bj = bias_ref[2 * pr:2 * pr + 2, :, j * ra:(j + 1) * ra].reshape(2 * ra, ra)
                bias.append(bj if blk_ok[j] is None else jnp.where(blk_ok[j], bj, NEG))
            sc = _dot_nt(qs, kcat) + jnp.concatenate(bias, axis=1)
            m = jnp.max(sc, axis=-1, keepdims=True)
            p = jnp.exp2(sc - m)
            denom = jnp.sum(p, axis=-1, keepdims=True)
            o = _dot(p.astype(BF16), vcat) / denom
            ycs.append(jnp.where(lane_c < HEAD_DIM, o[0:ra], o[ra:2 * ra]))
            if between is not None:
                between(sub * (ATT_HEADS // 2) + pr)
        yc.append(jnp.concatenate(ycs, axis=1))
    yc = jnp.concatenate(yc, axis=0).astype(BF16)

    _route(x1_buf[...], gf_ref, wr_ref, br_ref, hp_ref, gid_ref)

    d = D_MODEL
    merged = (gates_ref[:, 0:d].astype(F32) * _dot(ya, wa_ref[...])
              + gates_ref[:, d:2 * d].astype(F32) * _dot(yb, wb_ref[...])
              + gates_ref[:, 2 * d:3 * d].astype(F32) * _dot(yc, wc_ref[...]))
    x1 = x_ref[...] + _dot(merged.astype(BF16), wo_ref[...])
    o_ref[...] = x1
    x1_buf[...] = x1


def _mixer(l, x2d, a, u, v, q, k, vv, gates, c, seq):
    t = x2d.shape[0]
    r = ROWS_MIX
    nb = t // r
    prev = lambda w, n: pl.BlockSpec((r, w), lambda i: (jnp.maximum(jnp.minimum(i, nb - 1) - n, 0), 0))
    row = lambda w: prev(w, 0)
    routed = lambda shape: pl.BlockSpec(shape, lambda i: (jnp.maximum(i - 1, 0),) + (0,) * (len(shape) - 1))
    layer = [c["bd"], c["ps"], c["swc"], c["sbm"], c["bias"], c["wa"], c["wb"], c["wc"], c["wo"],
             c["gf"], c["wr"], c["br"]]
    in_specs = [row(D_MODEL), row(POOL_WIDTH), prev(POOL_WIDTH, 1), row(SGU_WIDTH), row(SGU_WIDTH),
                row(ATT_WIDTH),
                prev(ATT_WIDTH, 1), row(ATT_WIDTH),
                prev(ATT_WIDTH, 1), row(ATT_WIDTH),
                row(N_BRANCH * D_MODEL),
                _const_spec(c["invf"].shape), _const_spec(c["invr"].shape)] + [_layer_spec(w, l) for w in layer]
    return pl.pallas_call(
        functools.partial(_mixer_kernel, seq // r),
        grid=(nb + 1,),
        in_specs=in_specs,
        out_specs=[row(D_MODEL), routed((r * SUBLANES, LANES)), routed((1, 1, r))],
        out_shape=[jax.ShapeDtypeStruct((t, D_MODEL), F32), jax.ShapeDtypeStruct((t * SUBLANES, LANES), U32),
                   jax.ShapeDtypeStruct((nb, 1, r), I32)],
        scratch_shapes=[pltpu.VMEM((r + 2 * POOL_HALO, POOL_WIDTH), F32),
                        pltpu.VMEM((r, D_MODEL), F32)],
        compiler_params=_params(1),
        name="mixer",
    )(x2d, a, a, u, v, q, k, k, vv, vv, gates, c["invf"], c["invr"], *layer)


N_INPROJ_REFS = 7


def _front_kernel(blocks_per_seq, x_ref, *refs):
    proj_refs, refs = refs[:N_INPROJ_REFS], refs[N_INPROJ_REFS:]
    (a_buf, u_buf, v_buf, q_buf, k_buf, vv_buf, gates_buf, pool_buf, x1_buf) = refs[-9:]
    mix_refs, out_refs = refs[:-12], refs[-12:-9]
    r = ROWS_MIX
    step = pl.program_id(0)
    prev, cur = pl.ds(0, r), pl.ds(r, r)
    carried = (a_buf, k_buf, vv_buf)

    @pl.when(step == 0)
    def _():
        for buf in carried:
            buf[...] = jnp.zeros_like(buf)

    live = step < pl.num_programs(0) - 1
    for buf in carried:
        buf[prev, :] = jnp.where(live, buf[cur, :], buf[prev, :])

    branches, gate_segment = _inproj_parts(x_ref, *proj_refs)
    branches(a_buf.at[cur], u_buf, v_buf, q_buf, k_buf.at[cur], vv_buf.at[cur])

    def between(unit):
        if unit < N_GATE_SEGS:
            gate_segment(unit, gates_buf)

    _mixer_kernel(blocks_per_seq, x_ref, a_buf.at[cur], a_buf.at[prev], u_buf, v_buf, q_buf,
                  k_buf.at[prev], k_buf.at[cur], vv_buf.at[prev], vv_buf.at[cur], gates_buf,
                  *mix_refs, *out_refs, pool_buf, x1_buf, between=between)


def _front(l, x2d, ci, c, seq):
    t = x2d.shape[0]
    r = ROWS_MIX
    nb = t // r
    row = lambda w: pl.BlockSpec((r, w), lambda i: (jnp.minimum(i, nb - 1), 0))
    routed = lambda shape: pl.BlockSpec(shape, lambda i: (jnp.maximum(i - 1, 0),) + (0,) * (len(shape) - 1))
    proj = [ci["g"], ci["w_in"], ci["sgn"], ci["gq"], ci["gk"]]
    layer = [c["bd"], c["ps"], c["swc"], c["sbm"], c["bias"], c["wa"], c["wb"], c["wc"], c["wo"],
             c["gf"], c["wr"], c["br"]]
    in_specs = ([row(D_MODEL)] + [_layer_spec(w, l) for w in proj]
                + [_const_spec(ci["hm"].shape), _layer_spec(ci["gb"], l),
                   _const_spec(c["invf"].shape), _const_spec(c["invr"].shape)] + [_layer_spec(w, l) for w in layer])
    vmem = lambda rows, w, dt: pltpu.VMEM((rows, w), dt)
    return pl.pallas_call(
        functools.partial(_front_kernel, seq // r),
        grid=(nb + 1,),
        in_specs=in_specs,
        out_specs=[row(D_MODEL), routed((r * SUBLANES, LANES)), routed((1, 1, r))],
        out_shape=[jax.ShapeDtypeStruct((t, D_MODEL), F32), jax.ShapeDtypeStruct((t * SUBLANES, LANES), U32),
                   jax.ShapeDtypeStruct((nb, 1, r), I32)],
        scratch_shapes=[vmem(2 * r, POOL_WIDTH, BF16), vmem(r, SGU_WIDTH, BF16), vmem(r, SGU_WIDTH, BF16),
                        vmem(r, ATT_WIDTH, BF16), vmem(2 * r, ATT_WIDTH, BF16), vmem(2 * r, ATT_WIDTH, BF16),
                        vmem(r, N_BRANCH * D_MODEL, BF16),
                        vmem(r + 2 * POOL_HALO, POOL_WIDTH, F32), vmem(r, D_MODEL, F32)],
        compiler_params=_params(1),
        name="front",
    )(x2d, *proj, ci["hm"], ci["gb"], c["invf"], c["invr"], *layer)


ROUTER_ROWS = SUBLANES + N_GROUPS * EXPERTS_PER_GROUP


def _route(x, g_ref, wr_ref, br_ref, hp_ref, gid_ref):
    tm = x.shape[0]
    h = _rms(x, g_ref[...])
    h_hi = h.astype(BF16)
    h_lo = (h - h_hi.astype(F32)).astype(BF16)
    nr = ROUTER_ROWS
    l1 = _dot_nt(wr_ref[...], h_hi)
    l2 = _dot_nt(wr_ref[0:nr, :], h_lo)
    logits = l1[0:nr] + l1[nr:2 * nr] + l2 + br_ref[...]

    rows = lax.broadcasted_iota(I32, (SUBLANES, tm), 0).astype(F32)
    big = np.float32(1 << 20)
    gl = jnp.where(rows < N_GROUPS, logits[0:SUBLANES], NEG)
    gmax = jnp.max(gl, axis=0, keepdims=True)
    g_p = 1.0 / jnp.sum(jnp.exp(gl - gmax), axis=0, keepdims=True)
    g_idx = jnp.min(jnp.where(gl == gmax, rows, big), axis=0, keepdims=True)

    el = logits[SUBLANES:2 * SUBLANES]
    for gi in range(1, N_GROUPS):
        el = jnp.where(g_idx == gi, logits[SUBLANES * (gi + 1):SUBLANES * (gi + 2)], el)
    emax = jnp.max(el, axis=0, keepdims=True)
    esum = jnp.sum(jnp.exp(el - emax), axis=0, keepdims=True)
    i0 = jnp.min(jnp.where(el == emax, rows, big), axis=0, keepdims=True)
    el2 = jnp.where(rows == i0, -jnp.inf, el)
    m2 = jnp.max(el2, axis=0, keepdims=True)
    i1 = jnp.min(jnp.where(el2 == m2, rows, big), axis=0, keepdims=True)
    p0 = 1.0 / esum
    p1 = jnp.exp(m2 - emax) / esum
    w0 = g_p * p0 / (p0 + p1)
    w1 = g_p * p1 / (p0 + p1)
    gates = jnp.where(rows == i0, w0, 0.0) + jnp.where(rows == i1, w1, 0.0)

    gates_t = jnp.concatenate([gates, jnp.zeros((LANES - SUBLANES, tm), F32)], axis=0).T
    packed = _pack_pairs(h)
    for j in range(PACK_ROWS):
        hp_ref[_slab_rows(j, tm), :] = packed[:, j * LANES:(j + 1) * LANES]
    hp_ref[_slab_rows(PACK_ROWS, tm), :] = pltpu.bitcast(gates_t, U32)
    for j in range(PACK_ROWS + 1, SUBLANES):
        hp_ref[_slab_rows(j, tm), :] = jnp.zeros((tm, LANES), U32)
    gid_ref[...] = g_idx.astype(I32).reshape(1, 1, tm)


def _moe_kernel(padded, spad_ref, poffs_ref, ntiles_ref, hp_ref, win_ref, wout_ref, o_ref, xt_ref, yt_ref):
    c = pl.program_id(0)
    g = pl.program_id(1)
    seg0 = c * padded + poffs_ref[c * N_GROUPS + g]
    tr = MOE_TILE
    st = MOE_STRIDE

    def tile_body(t, carry):
        r0 = seg0 + t * tr

        def slab(entry):
            return pl.ds(pl.multiple_of(spad_ref[entry], SUBLANES), SUBLANES)

        def gather(i, cr):
            rr, entry = i * MOE_UNROLL, r0 + i * MOE_UNROLL
            for uu in range(MOE_UNROLL):
                xt_ref[pl.ds(rr + uu, SUBLANES, stride=st), :] = hp_ref[slab(entry + uu), :]
            return cr

        lax.fori_loop(0, tr // MOE_UNROLL, gather, 0)

        packed = jnp.concatenate([xt_ref[pl.ds(j * st, tr), :] for j in range(PACK_ROWS)], axis=1)
        xb = _unpack_pairs(packed).astype(BF16)
        gates = pltpu.bitcast(xt_ref[pl.ds(PACK_ROWS * st, tr), :], F32)
        acts = []
        for e in range(EXPERTS_PER_GROUP):
            hu = _dot(xb, win_ref[e])
            hg = hu[:, 0:D_EXPERT]
            acts.append((hg * _sigmoid(hg) * hu[:, D_EXPERT:] * gates[:, e:e + 1]).astype(BF16))
        y = _dot(jnp.concatenate(acts, axis=1), wout_ref[...].reshape(GROUP_FF, D_MODEL))
        for j in range(SUBLANES):
            yt_ref[pl.ds(j * st, tr), :] = y[:, j * LANES:(j + 1) * LANES]

        def scatter(i, cr):
            rr, entry = i * MOE_UNROLL, r0 + i * MOE_UNROLL
            for uu in range(MOE_UNROLL):
                o_ref[slab(entry + uu), :] = yt_ref[pl.ds(rr + uu, SUBLANES, stride=st), :]
            return cr

        lax.fori_loop(0, tr // MOE_UNROLL, scatter, 0)
        return carry

    lax.fori_loop(0, ntiles_ref[c * N_GROUPS + g], tile_body, 0)


def _moe(l, hp, spad, poffs, ntiles, c, chunk):
    t = hp.shape[0] // SUBLANES
    win, wout = c["win"], c["wout"]
    grid_spec = pltpu.PrefetchScalarGridSpec(
        num_scalar_prefetch=3,
        grid=(t // chunk, N_GROUPS),
        in_specs=[pl.BlockSpec((chunk * SUBLANES, LANES), lambda ci, g, *_: (ci, 0)),
                  pl.BlockSpec((None, EXPERTS_PER_GROUP) + win.shape[2:], lambda ci, g, *_: (l, g, 0, 0)),
                  pl.BlockSpec((None, EXPERTS_PER_GROUP) + wout.shape[2:], lambda ci, g, *_: (l, g, 0, 0))],
        out_specs=pl.BlockSpec((chunk * SUBLANES, LANES), lambda ci, g, *_: (ci, 0)),
        scratch_shapes=[pltpu.VMEM((MOE_STRIDE * SUBLANES, LANES), U32),
                        pltpu.VMEM((MOE_STRIDE * SUBLANES, LANES), F32)],
    )
    return pl.pallas_call(
        functools.partial(_moe_kernel, spad.shape[0] // (t // chunk)),
        grid_spec=grid_spec,
        out_shape=jax.ShapeDtypeStruct((t * SUBLANES, LANES), F32),
        compiler_params=_params(2),
        name="moe",
    )(spad, poffs, ntiles, hp, win, wout)


def _ple_kernel(x_ref, mo_ref, p_ref, g_ref, wpi_ref, wg_ref, o_ref):
    tm = x_ref.shape[0]
    mo = jnp.concatenate([mo_ref[_slab_rows(j, tm), :] for j in range(SUBLANES)], axis=1)
    x = x_ref[...] + mo
    hn = _rms(x, g_ref[...]).astype(BF16)
    gate = _sigmoid(_dot(hn, wg_ref[...]))
    emb = _dot(p_ref[...].astype(BF16), wpi_ref[...])
    o_ref[...] = x + emb * gate


def _ple(l, x2d, mo, p3d, c):
    t = x2d.shape[0]
    tm = ROWS_PROJ
    row = lambda w: pl.BlockSpec((tm, w), lambda i: (i, 0))
    layer = [c["g"], c["wpi"], c["wg"]]
    return pl.pallas_call(
        _ple_kernel,
        grid=(t // tm,),
        in_specs=[row(D_MODEL), pl.BlockSpec((tm * SUBLANES, LANES), lambda i: (i, 0)),
                  pl.BlockSpec((None, tm, PLE_DIM), lambda i: (l, i, 0))] + [_layer_spec(w, l) for w in layer],
        out_specs=row(D_MODEL),
        out_shape=jax.ShapeDtypeStruct((t, D_MODEL), F32),
        compiler_params=_params(1),
        name="ple",
    )(x2d, mo, p3d, *layer)


def _attention_bias(rel_bias):
    r = ROWS_ATT
    nk = (HALO_BLOCKS + 1) * r
    qi = np.arange(r)[:, None]
    kj = np.arange(nk)[None, :]
    qc = qi // CHUNK
    kc = kj // CHUNK
    valid = (kc >= qc) & (kc <= qc + LEFT_CHUNKS)
    p = nk + r
    off = np.arange(p)
    off = np.where(off < nk, off, off - p)
    rel = np.clip(HALO_BLOCKS * r - off, -REL_CLIP, REL_CLIP) + REL_CLIP
    per_off = rel_bias[..., rel].astype(F32)
    lead = per_off.shape[:-1]
    band = jnp.tile(per_off, (1, 1, r))[..., :r * (p - 1)].reshape(lead + (r, p - 1))[..., :nk]
    return jnp.where(jnp.asarray(valid), band * np.float32(LOG2E), NEG)


def _pool_inverse_counts():
    r = ROWS_MIX
    win = np.repeat(np.asarray(POOL_WINDOWS, np.float32), POOL_GROUP_DIM)[None, :]
    pos = np.arange(1, r + 1, dtype=np.float32)[:, None]
    return jnp.asarray(1.0 / np.minimum(pos, win), F32), jnp.asarray(1.0 / win, F32)


def _block_diag(w):
    nl, g, c, d = w.shape
    eye = jnp.eye(g, dtype=w.dtype)
    return jnp.einsum("gh,lgcd->lgchd", eye, w).reshape(nl, g * c, g * d)


def _prepare(prm):
    nl = prm["w_in"].shape[0]
    row = lambda v: v.reshape(nl, 1, -1).astype(F32)
    hm = _block_diag(jnp.full((1, MXU_DIM // HEAD_DIM, HEAD_DIM, HEAD_DIM), 1.0 / HEAD_DIM, F32))[0].astype(BF16)
    inproj = dict(
        g=row(prm["mix_norm_g"]), w_in=prm["w_in"].astype(BF16), sgn=row(prm["sgu_norm_g"]),
        gq=row(jnp.tile(prm["q_norm_g"], (1, ATT_HEADS))), gk=row(jnp.tile(prm["k_norm_g"], (1, ATT_HEADS))),
        hm=hm, gb=row(prm["gate_b"]))

    i = np.arange(SGU_BLOCK)
    tri = (i[None, :] // CHUNK) <= (i[:, None] // CHUNK)
    sw = jnp.where(jnp.asarray(tri), prm["sgu_w"], 0.0)
    swc = jnp.concatenate([sw[:, g] for g in range(SGU_GROUPS)], axis=2).astype(BF16)
    sbm = jnp.repeat(jnp.swapaxes(prm["sgu_b"], 1, 2), SGU_GROUP_DIM, axis=2).astype(F32)
    invf, invr = _pool_inverse_counts()
    mixer = dict(invf=invf, invr=invr, bd=_block_diag(prm["pool_w"]).astype(BF16), ps=row(prm["pool_scale"]),
                 swc=swc, sbm=sbm, bias=_attention_bias(prm["rel_bias"]),
                 wa=prm["w_branch_a"].astype(BF16), wb=prm["w_branch_b"].astype(BF16),
                 wc=prm["w_branch_c"].astype(BF16), wo=prm["w_out"].astype(BF16))

    pad = jnp.zeros((nl, SUBLANES - N_GROUPS, D_MODEL), F32)
    wr = jnp.concatenate([jnp.swapaxes(prm["w_group_router"], 1, 2), pad,
                          jnp.swapaxes(prm["w_expert_router"], 1, 2)], axis=1)
    wr_hi = wr.astype(BF16)
    wr_lo = (wr - wr_hi.astype(F32)).astype(BF16)
    br = jnp.concatenate([prm["b_group_router"], jnp.zeros((nl, SUBLANES - N_GROUPS), F32),
                          prm["b_expert_router"]], axis=1).reshape(nl, -1, 1).astype(F32)
    mixer.update(gf=row(prm["ffn_norm_g"]), wr=jnp.concatenate([wr_hi, wr_lo], axis=1), br=br)

    moe = dict(win=prm["w_expert_in"].astype(BF16),
               wout=prm["w_expert_out"].astype(BF16))

    ple = dict(g=row(prm["ple_norm_g"]), wpi=prm["w_ple_in"].astype(BF16), wg=prm["w_ple_gate"].astype(BF16))
    return inproj, mixer, moe, ple


def _sort_by_group(gid, chunk):
    tr = MOE_TILE
    gid = gid.reshape(-1, chunk)
    order = jnp.argsort(gid, axis=1, stable=True).astype(I32)
    groups = jnp.arange(N_GROUPS, dtype=I32)
    cnts = jnp.sum(gid[:, :, None] == groups, axis=1).astype(I32)
    offs = jnp.cumsum(cnts, axis=1) - cnts
    ntiles = (cnts + (tr - 1)) // tr
    poffs = (jnp.cumsum(ntiles, axis=1) - ntiles) * tr
    slot = jnp.arange(chunk + N_GROUPS * tr, dtype=I32)[None, :, None]
    seg = jnp.sum(slot >= poffs[:, None, 1:], axis=2)
    pick = lambda v: jnp.sum(jnp.where(seg[:, :, None] == groups, v[:, None, :], 0), axis=2)
    src = pick(offs) + jnp.clip(slot[:, :, 0] - pick(poffs), 0, jnp.maximum(pick(cnts) - 1, 0))
    spad = jnp.take_along_axis(order, jnp.minimum(src, chunk - 1), axis=1) * SUBLANES
    return spad.reshape(-1), poffs.reshape(-1).astype(I32), ntiles.reshape(-1).astype(I32)


def kernel(x, p, mix_norm_g, w_in, pool_w, pool_scale, sgu_norm_g, sgu_w, sgu_b, q_norm_g, k_norm_g, rel_bias, gate_b, w_branch_a, w_branch_b, w_branch_c, w_out, ffn_norm_g, w_group_router, b_group_router, w_expert_router, b_expert_router, w_expert_in, w_expert_out, ple_norm_g, w_ple_in, w_ple_gate):
    prm = dict(mix_norm_g=mix_norm_g, w_in=w_in, pool_w=pool_w, pool_scale=pool_scale, sgu_norm_g=sgu_norm_g,
               sgu_w=sgu_w, sgu_b=sgu_b, q_norm_g=q_norm_g, k_norm_g=k_norm_g, rel_bias=rel_bias, gate_b=gate_b,
               w_branch_a=w_branch_a, w_branch_b=w_branch_b, w_branch_c=w_branch_c, w_out=w_out,
               ffn_norm_g=ffn_norm_g, w_group_router=w_group_router, b_group_router=b_group_router,
               w_expert_router=w_expert_router, b_expert_router=b_expert_router, w_expert_in=w_expert_in,
               w_expert_out=w_expert_out, ple_norm_g=ple_norm_g, w_ple_in=w_ple_in, w_ple_gate=w_ple_gate)
    b, s, d = x.shape
    t = b * s
    depth = p.shape[0]
    chunk = min(MOE_CHUNK, t)
    assert d == D_MODEL and s % ROWS_MIX == 0 and t % ROWS_PROJ == 0 and t % ROWS_INPROJ == 0 and t % chunk == 0
    assert ROWS_MIX % ROWS_ATT == 0 and ROWS_MIX // ROWS_ATT >= HALO_BLOCKS
    c_in, c_mix, c_moe, c_ple = _prepare(prm)
    p3d = p.reshape(depth, t, PLE_DIM)
    xf = x.reshape(t, d)
    for l in range(depth):
        xf, hp, gid = _front(l, xf, c_in, c_mix, s)
        spad, poffs, ntiles = _sort_by_group(gid, chunk)
        mo = _moe(l, hp, spad, poffs, ntiles, c_moe, chunk)
        xf = _ple(l, xf, mo, p3d, c_ple)
    return xf.reshape(b, s, d)
```

```python
import functools

import jax
import jax.numpy as jnp
import numpy as np
from jax import lax
from jax.experimental import pallas as pl
from jax.experimental.pallas import tpu as pltpu

F32 = jnp.float32
BF16 = jnp.bfloat16
U32 = jnp.uint32
I32 = jnp.int32

D_MODEL = 1024
CHUNK = 64
PLE_DIM = 256
EPS = 1e-6
POOL_WINDOWS = (2, 4, 8, 16)
POOL_WIDTH = 256
POOL_GROUP_DIM = 64
SGU_BLOCK = 128
SGU_WIDTH = 256
SGU_GROUPS = 4
SGU_GROUP_DIM = 64
HEAD_DIM = 64
ATT_WIDTH = 512
ATT_HEADS = 8
LEFT_CHUNKS = 8
REL_CLIP = 128
N_BRANCH = 3
N_GROUPS = 4
EXPERTS_PER_GROUP = 8
D_EXPERT = 128
GROUP_FF = EXPERTS_PER_GROUP * D_EXPERT

C_A, C_U, C_V, C_Q, C_K, C_VV, C_G = 0, 256, 512, 768, 1280, 1792, 2304

LANES = 128
SUBLANES = 8
MXU_DIM = 256
VMEM_LIMIT = 56 * 1024 * 1024

ROWS_PROJ = 1024
ROWS_MIX = 512
ROWS_ATT = 256
HALO_BLOCKS = 2
POOL_HALO = 16
MOE_CHUNK = 2048
MOE_TILE = 256
MOE_STRIDE = MOE_TILE + 1
MOE_UNROLL = 32
PACK_W = D_MODEL // 2
PACK_ROWS = PACK_W // LANES
NEG = -1e30
GATE_SEG = 512
N_GATE_SEGS = N_BRANCH * D_MODEL // GATE_SEG
LOG2E = float(np.log2(np.e))


def _const_spec(shape):
    nd = len(shape)
    return pl.BlockSpec(shape, lambda *_: (0,) * nd, pipeline_mode=pl.Buffered(1))


def _layer_spec(arr, l):
    nd = arr.ndim - 1
    return pl.BlockSpec((None,) + arr.shape[1:], lambda *_: (l,) + (0,) * nd, pipeline_mode=pl.Buffered(1))


def _params(n_axes):
    return pltpu.CompilerParams(dimension_semantics=("arbitrary",) * n_axes,
                                vmem_limit_bytes=VMEM_LIMIT)


def _rms(x, g):
    ms = jnp.mean(x * x, axis=-1, keepdims=True)
    return x * lax.rsqrt(ms + EPS) * g


def _gelu_tanh(x):
    c = np.float32(np.sqrt(2.0 / np.pi))
    return 0.5 * x * (1.0 + jnp.tanh(c * (x + np.float32(0.044715) * (x * x * x))))


def _sigmoid(x):
    return 0.5 * jnp.tanh(0.5 * x) + 0.5


def _dot(a, b):
    return jnp.dot(a, b, preferred_element_type=F32)


def _dot_nt(a, b):
    return lax.dot_general(a, b, (((1,), (1,)), ((), ())), preferred_element_type=F32)


def _pack_pairs(x):
    w = x.shape[1] // 2
    bits = pltpu.bitcast(x.astype(BF16).astype(F32), U32)
    return (bits[:, :w] >> 16) | (bits[:, w:] & np.uint32(0xFFFF0000))


def _unpack_pairs(p):
    lo = pltpu.bitcast(p << 16, F32)
    hi = pltpu.bitcast(p & np.uint32(0xFFFF0000), F32)
    return jnp.concatenate([lo, hi], axis=1)


def _slab_rows(j, n):
    return pl.ds(j, n, stride=SUBLANES)


def _inproj_parts(x_ref, g_ref, w_ref, sgn_ref, gq_ref, gk_ref, hm_ref, gb_ref):
    h = _rms(x_ref[...], g_ref[...]).astype(BF16)

    def seg(c0, c1):
        return _dot(h, w_ref[:, c0:c1])

    def head_norm(z, g):
        zz = (z * z).astype(BF16)
        w = hm_ref.shape[0]
        ms = jnp.concatenate([_dot(zz[:, c0:c0 + w], hm_ref[...]) for c0 in range(0, ATT_WIDTH, w)], axis=1)
        return z * lax.rsqrt(ms + EPS) * g

    def branches(a_ref, u_ref, v_ref, q_ref, k_ref, vv_ref):
        a_ref[...] = seg(C_A, C_U).astype(BF16)
        u_ref[...] = _gelu_tanh(seg(C_U, C_V)).astype(BF16)
        v_ref[...] = _rms(_gelu_tanh(seg(C_V, C_Q)), sgn_ref[...]).astype(BF16)
        q_ref[...] = (head_norm(seg(C_Q, C_K), gq_ref[...]) * np.float32(HEAD_DIM ** -0.5 * LOG2E)).astype(BF16)
        k_ref[...] = head_norm(seg(C_K, C_VV), gk_ref[...]).astype(BF16)
        vv_ref[...] = seg(C_VV, C_G).astype(BF16)

    def gate_segment(i, gates_ref):
        c0 = i * GATE_SEG
        z = seg(C_G + c0, C_G + c0 + GATE_SEG) + gb_ref[:, c0:c0 + GATE_SEG]
        gates_ref[:, c0:c0 + GATE_SEG] = _sigmoid(z).astype(BF16)

    return branches, gate_segment


def _mixer_kernel(blocks_per_seq,
                  x_ref, a_ref, ap_ref, u_ref, v_ref, q_ref,
                  kp_ref, kc_ref, vp_ref, vc_ref, gates_ref,
                  invf_ref, invr_ref, bd_ref, ps_ref, swc_ref, sb_ref, bias_ref,
                  wa_ref, wb_ref, wc_ref, wo_ref, gf_ref, wr_ref, br_ref,
                  o_ref, hp_ref, gid_ref, pool_buf, x1_buf, between=None):
    r = ROWS_MIX
    step = pl.program_id(0)
    bi = step % blocks_per_seq
    first = bi == 0

    @pl.when(step == 0)
    def _():
        x1_buf[...] = jnp.zeros_like(x1_buf)

    a = a_ref[...].astype(F32)
    halo = jnp.where(first, 0.0, ap_ref[r - POOL_HALO:, :].astype(F32))
    pool_buf[0:POOL_HALO, :] = jnp.zeros((POOL_HALO, POOL_WIDTH), F32)
    pool_buf[POOL_HALO:2 * POOL_HALO, :] = halo
    pool_buf[2 * POOL_HALO:, :] = a
    lane_a = lax.broadcasted_iota(I32, (r, POOL_WIDTH), 1) // POOL_GROUP_DIM
    wsum = None
    for gi, k in enumerate((1, 2, 4, 8)):
        cur = pool_buf[POOL_HALO:, :] + pool_buf[POOL_HALO - k:r + 2 * POOL_HALO - k, :]
        pool_buf[POOL_HALO:, :] = cur
        level = cur[POOL_HALO:, :]
        wsum = level if wsum is None else jnp.where(lane_a >= gi, level, wsum)
    inv = jnp.where(first, invf_ref[...], invr_ref[...])
    pooled = (wsum * inv - a).astype(BF16)
    ya = (_dot(pooled, bd_ref[...]) * ps_ref[...]).astype(BF16)

    vb = v_ref[...]
    lane_b = lax.broadcasted_iota(I32, (SGU_BLOCK, SGU_WIDTH), 1) // SGU_GROUP_DIM
    zero_b = jnp.zeros((SGU_BLOCK, SGU_WIDTH), BF16)
    mixed = []
    for s in range(r // SGU_BLOCK):
        blk = vb[s * SGU_BLOCK:(s + 1) * SGU_BLOCK, :]
        rhs = jnp.concatenate([jnp.where(lane_b == g, blk, zero_b) for g in range(SGU_GROUPS)], axis=0)
        mixed.append(_dot(swc_ref[...], rhs) + sb_ref[...])
    yb = (u_ref[...].astype(F32) * jnp.concatenate(mixed, axis=0)).astype(BF16)

    ra = ROWS_ATT
    nsub = r // ra
    nblk = HALO_BLOCKS + 1
    lane_c = lax.broadcasted_iota(I32, (ra, LANES), 1)
    zero_q = jnp.zeros((ra, LANES), BF16)
    yc = []
    for sub in range(nsub):
        lo = (nsub + sub - HALO_BLOCKS) * ra
        sub_idx = bi * nsub + sub
        blk_ok = [sub_idx >= HALO_BLOCKS - j for j in range(HALO_BLOCKS)] + [None]
        ycs = []
        for pr in range(ATT_HEADS // 2):
            cols = slice(pr * LANES, (pr + 1) * LANES)
            qp = q_ref[sub * ra:(sub + 1) * ra, cols]
            qs = jnp.concatenate([jnp.where(lane_c < HEAD_DIM, qp, zero_q),
                                  jnp.where(lane_c >= HEAD_DIM, qp, zero_q)], axis=0)
            kcat = jnp.concatenate([kp_ref[lo:, cols], kc_ref[0:(sub + 1) * ra, cols]], axis=0)
            vcat = jnp.concatenate([vp_ref[lo:, cols], vc_ref[0:(sub + 1) * ra, cols]], axis=0)
            bias = []
            for j in range(nblk):
                bj = bias_ref[2 * pr:2 * pr + 2, :, j * ra:(j + 1) * ra].reshape(2 * ra, ra)
                bias.append(bj if blk_ok[j] is None else jnp.where(blk_ok[j], bj, NEG))
            sc = _dot_nt(qs, kcat) + jnp.concatenate(bias, axis=1)
            m = jnp.max(sc, axis=-1, keepdims=True)
            p = jnp.exp2(sc - m)
            denom = jnp.sum(p, axis=-1, keepdims=True)
            o = _dot(p.astype(BF16), vcat) / denom
            ycs.append(jnp.where(lane_c < HEAD_DIM, o[0:ra], o[ra:2 * ra]))
            if between is not None:
                between(sub * (ATT_HEADS // 2) + pr)
        yc.append(jnp.concatenate(ycs, axis=1))
    yc = jnp.concatenate(yc, axis=0).astype(BF16)

    _route(x1_buf[...], gf_ref, wr_ref, br_ref, hp_ref, gid_ref)

    d = D_MODEL
    merged = (gates_ref[:, 0:d].astype(F32) * _dot(ya, wa_ref[...])
              + gates_ref[:, d:2 * d].astype(F32) * _dot(yb, wb_ref[...])
              + gates_ref[:, 2 * d:3 * d].astype(F32) * _dot(yc, wc_ref[...]))
    x1 = x_ref[...] + _dot(merged.astype(BF16), wo_ref[...])
    o_ref[...] = x1
    x1_buf[...] = x1


N_INPROJ_REFS = 7


def _front_kernel(blocks_per_seq, x_ref, *refs):
    proj_refs, refs = refs[:N_INPROJ_REFS], refs[N_INPROJ_REFS:]
    (a_buf, u_buf, v_buf, q_buf, k_buf, vv_buf, gates_buf, pool_buf, x1_buf) = refs[-9:]
    mix_refs, out_refs = refs[:-12], refs[-12:-9]
    r = ROWS_MIX
    step = pl.program_id(0)
    prev, cur = pl.ds(0, r), pl.ds(r, r)
    carried = (a_buf, k_buf, vv_buf)

    @pl.when(step == 0)
    def _():
        for buf in carried:
            buf[...] = jnp.zeros_like(buf)

    live = step < pl.num_programs(0) - 1

    @pl.when(live)
    def _():
        for buf in carried:
            buf[prev, :] = buf[cur, :]
        branches, gate_segment = _inproj_parts(x_ref, *proj_refs)
        branches(a_buf.at[cur], u_buf, v_buf, q_buf, k_buf.at[cur], vv_buf.at[cur])

        def between(unit):
            if unit < N_GATE_SEGS:
                gate_segment(unit, gates_buf)

        _mixer_kernel(blocks_per_seq, x_ref, a_buf.at[cur], a_buf.at[prev], u_buf, v_buf, q_buf,
                      k_buf.at[prev], k_buf.at[cur], vv_buf.at[prev], vv_buf.at[cur], gates_buf,
                      *mix_refs, *out_refs, pool_buf, x1_buf, between=between)

    @pl.when(jnp.logical_not(live))
    def _():
        _route(x1_buf[...], *mix_refs[-3:], *out_refs[1:])


def _front(l, x2d, ci, c, seq):
    t = x2d.shape[0]
    r = ROWS_MIX
    nb = t // r
    row = lambda w: pl.BlockSpec((r, w), lambda i: (jnp.minimum(i, nb - 1), 0))
    routed = lambda shape: pl.BlockSpec(shape, lambda i: (jnp.maximum(i - 1, 0),) + (0,) * (len(shape) - 1))
    proj = [ci["g"], ci["w_in"], ci["sgn"], ci["gq"], ci["gk"]]
    layer = [c["bd"], c["ps"], c["swc"], c["sbm"], c["bias"], c["wa"], c["wb"], c["wc"], c["wo"],
             c["gf"], c["wr"], c["br"]]
    in_specs = ([row(D_MODEL)] + [_layer_spec(w, l) for w in proj]
                + [_const_spec(ci["hm"].shape), _layer_spec(ci["gb"], l),
                   _const_spec(c["invf"].shape), _const_spec(c["invr"].shape)] + [_layer_spec(w, l) for w in layer])
    vmem = lambda rows, w, dt: pltpu.VMEM((rows, w), dt)
    return pl.pallas_call(
        functools.partial(_front_kernel, seq // r),
        grid=(nb + 1,),
        in_specs=in_specs,
        out_specs=[row(D_MODEL), routed((r * SUBLANES, LANES)), routed((1, 1, r))],
        out_shape=[jax.ShapeDtypeStruct((t, D_MODEL), F32), jax.ShapeDtypeStruct((t * SUBLANES, LANES), U32),
                   jax.ShapeDtypeStruct((nb, 1, r), I32)],
        scratch_shapes=[vmem(2 * r, POOL_WIDTH, BF16), vmem(r, SGU_WIDTH, BF16), vmem(r, SGU_WIDTH, BF16),
                        vmem(r, ATT_WIDTH, BF16), vmem(2 * r, ATT_WIDTH, BF16), vmem(2 * r, ATT_WIDTH, BF16),
                        vmem(r, N_BRANCH * D_MODEL, BF16),
                        vmem(r + 2 * POOL_HALO, POOL_WIDTH, F32), vmem(r, D_MODEL, F32)],
        compiler_params=_params(1),
        name="front",
    )(x2d, *proj, ci["hm"], ci["gb"], c["invf"], c["invr"], *layer)


ROUTER_ROWS = SUBLANES + N_GROUPS * EXPERTS_PER_GROUP


def _route(x, g_ref, wr_ref, br_ref, hp_ref, gid_ref):
    tm = x.shape[0]
    h = _rms(x, g_ref[...])
    h_hi = h.astype(BF16)
    h_lo = (h - h_hi.astype(F32)).astype(BF16)
    nr = ROUTER_ROWS
    l1 = _dot_nt(wr_ref[...], h_hi)
    l2 = _dot_nt(wr_ref[0:nr, :], h_lo)
    logits = l1[0:nr] + l1[nr:2 * nr] + l2 + br_ref[...]

    rows = lax.broadcasted_iota(I32, (SUBLANES, tm), 0).astype(F32)
    big = np.float32(1 << 20)
    gl = jnp.where(rows < N_GROUPS, logits[0:SUBLANES], NEG)
    gmax = jnp.max(gl, axis=0, keepdims=True)
    g_p = 1.0 / jnp.sum(jnp.exp(gl - gmax), axis=0, keepdims=True)
    g_idx = jnp.min(jnp.where(gl == gmax, rows, big), axis=0, keepdims=True)

    el = logits[SUBLANES:2 * SUBLANES]
    for gi in range(1, N_GROUPS):
        el = jnp.where(g_idx == gi, logits[SUBLANES * (gi + 1):SUBLANES * (gi + 2)], el)
    emax = jnp.max(el, axis=0, keepdims=True)
    esum = jnp.sum(jnp.exp(el - emax), axis=0, keepdims=True)
    i0 = jnp.min(jnp.where(el == emax, rows, big), axis=0, keepdims=True)
    el2 = jnp.where(rows == i0, -jnp.inf, el)
    m2 = jnp.max(el2, axis=0, keepdims=True)
    i1 = jnp.min(jnp.where(el2 == m2, rows, big), axis=0, keepdims=True)
    p0 = 1.0 / esum
    p1 = jnp.exp(m2 - emax) / esum
    w0 = g_p * p0 / (p0 + p1)
    w1 = g_p * p1 / (p0 + p1)
    gates = jnp.where(rows == i0, w0, 0.0) + jnp.where(rows == i1, w1, 0.0)

    gates_t = jnp.concatenate([gates, jnp.zeros((LANES - SUBLANES, tm), F32)], axis=0).T
    packed = _pack_pairs(h)
    for j in range(PACK_ROWS):
        hp_ref[_slab_rows(j, tm), :] = packed[:, j * LANES:(j + 1) * LANES]
    hp_ref[_slab_rows(PACK_ROWS, tm), :] = pltpu.bitcast(gates_t, U32)
    for j in range(PACK_ROWS + 1, SUBLANES):
        hp_ref[_slab_rows(j, tm), :] = jnp.zeros((tm, LANES), U32)
    gid_ref[...] = g_idx.astype(I32).reshape(1, 1, tm)


def _moe_kernel(padded, spad_ref, poffs_ref, ntiles_ref, hp_ref, win_ref, wout_ref, o_ref, xt_ref, yt_ref):
    c = pl.program_id(0)
    g = pl.program_id(1)
    seg0 = c * padded + poffs_ref[c * N_GROUPS + g]
    tr = MOE_TILE
    st = MOE_STRIDE

    def tile_body(t, carry):
        r0 = seg0 + t * tr

        def slab(entry):
            return pl.ds(pl.multiple_of(spad_ref[entry], SUBLANES), SUBLANES)

        def gather(i, cr):
            rr, entry = i * MOE_UNROLL, r0 + i * MOE_UNROLL
            for uu in range(MOE_UNROLL):
                xt_ref[pl.ds(rr + uu, SUBLANES, stride=st), :] = hp_ref[slab(entry + uu), :]
            return cr

        lax.fori_loop(0, tr // MOE_UNROLL, gather, 0)

        packed = jnp.concatenate([xt_ref[pl.ds(j * st, tr), :] for j in range(PACK_ROWS)], axis=1)
        xb = _unpack_pairs(packed).astype(BF16)
        gates = pltpu.bitcast(xt_ref[pl.ds(PACK_ROWS * st, tr), :], F32)
        acts = []
        for e in range(EXPERTS_PER_GROUP):
            hu = _dot(xb, win_ref[e])
            hg = hu[:, 0:D_EXPERT]
            acts.append((hg * _sigmoid(hg) * hu[:, D_EXPERT:] * gates[:, e:e + 1]).astype(BF16))
        y = _dot(jnp.concatenate(acts, axis=1), wout_ref[...].reshape(GROUP_FF, D_MODEL))
        for j in range(SUBLANES):
            yt_ref[pl.ds(j * st, tr), :] = y[:, j * LANES:(j + 1) * LANES]

        def scatter(i, cr):
            rr, entry = i * MOE_UNROLL, r0 + i * MOE_UNROLL
            for uu in range(MOE_UNROLL):
                o_ref[slab(entry + uu), :] = yt_ref[pl.ds(rr + uu, SUBLANES, stride=st), :]
            return cr

        lax.fori_loop(0, tr // MOE_UNROLL, scatter, 0)
        return carry

    lax.fori_loop(0, ntiles_ref[c * N_GROUPS + g], tile_body, 0)


def _moe(l, hp, spad, poffs, ntiles, c, chunk):
    t = hp.shape[0] // SUBLANES
    win, wout = c["win"], c["wout"]
    grid_spec = pltpu.PrefetchScalarGridSpec(
        num_scalar_prefetch=3,
        grid=(t // chunk, N_GROUPS),
        in_specs=[pl.BlockSpec((chunk * SUBLANES, LANES), lambda ci, g, *_: (ci, 0)),
                  pl.BlockSpec((None, EXPERTS_PER_GROUP) + win.shape[2:], lambda ci, g, *_: (l, g, 0, 0)),
                  pl.BlockSpec((None, EXPERTS_PER_GROUP) + wout.shape[2:], lambda ci, g, *_: (l, g, 0, 0))],
        out_specs=pl.BlockSpec((chunk * SUBLANES, LANES), lambda ci, g, *_: (ci, 0)),
        scratch_shapes=[pltpu.VMEM((MOE_STRIDE * SUBLANES, LANES), U32),
                        pltpu.VMEM((MOE_STRIDE * SUBLANES, LANES), F32)],
    )
    return pl.pallas_call(
        functools.partial(_moe_kernel, spad.shape[0] // (t // chunk)),
        grid_spec=grid_spec,
        out_shape=jax.ShapeDtypeStruct((t * SUBLANES, LANES), F32),
        compiler_params=_params(2),
        name="moe",
    )(spad, poffs, ntiles, hp, win, wout)


def _ple_kernel(x_ref, mo_ref, p_ref, g_ref, wpi_ref, wg_ref, o_ref):
    tm = x_ref.shape[0]
    mo = jnp.concatenate([mo_ref[_slab_rows(j, tm), :] for j in range(SUBLANES)], axis=1)
    x = x_ref[...] + mo
    hn = _rms(x, g_ref[...]).astype(BF16)
    gate = _sigmoid(_dot(hn, wg_ref[...]))
    emb = _dot(p_ref[...].astype(BF16), wpi_ref[...])
    o_ref[...] = x + emb * gate


def _ple(l, x2d, mo, p3d, c):
    t = x2d.shape[0]
    tm = ROWS_PROJ
    row = lambda w: pl.BlockSpec((tm, w), lambda i: (i, 0))
    layer = [c["g"], c["wpi"], c["wg"]]
    return pl.pallas_call(
        _ple_kernel,
        grid=(t // tm,),
        in_specs=[row(D_MODEL), pl.BlockSpec((tm * SUBLANES, LANES), lambda i: (i, 0)),
                  pl.BlockSpec((None, tm, PLE_DIM), lambda i: (l, i, 0))] + [_layer_spec(w, l) for w in layer],
        out_specs=row(D_MODEL),
        out_shape=jax.ShapeDtypeStruct((t, D_MODEL), F32),
        compiler_params=_params(1),
        name="ple",
    )(x2d, mo, p3d, *layer)


def _attention_bias(rel_bias):
    r = ROWS_ATT
    nk = (HALO_BLOCKS + 1) * r
    qi = np.arange(r)[:, None]
    kj = np.arange(nk)[None, :]
    qc = qi // CHUNK
    kc = kj // CHUNK
    valid = (kc >= qc) & (kc <= qc + LEFT_CHUNKS)
    p = nk + r
    off = np.arange(p)
    off = np.where(off < nk, off, off - p)
    rel = np.clip(HALO_BLOCKS * r - off, -REL_CLIP, REL_CLIP) + REL_CLIP
    per_off = rel_bias[..., rel].astype(F32)
    lead = per_off.shape[:-1]
    band = jnp.tile(per_off, (1, 1, r))[..., :r * (p - 1)].reshape(lead + (r, p - 1))[..., :nk]
    return jnp.where(jnp.asarray(valid), band * np.float32(LOG2E), NEG)


def _pool_inverse_counts():
    r = ROWS_MIX
    win = np.repeat(np.asarray(POOL_WINDOWS, np.float32), POOL_GROUP_DIM)[None, :]
    pos = np.arange(1, r + 1, dtype=np.float32)[:, None]
    return jnp.asarray(1.0 / np.minimum(pos, win), F32), jnp.asarray(1.0 / win, F32)


def _block_diag(w):
    nl, g, c, d = w.shape
    eye = jnp.eye(g, dtype=w.dtype)
    return jnp.einsum("gh,lgcd->lgchd", eye, w).reshape(nl, g * c, g * d)


def _prepare(prm):
    nl = prm["w_in"].shape[0]
    row = lambda v: v.reshape(nl, 1, -1).astype(F32)
    hm = _block_diag(jnp.full((1, MXU_DIM // HEAD_DIM, HEAD_DIM, HEAD_DIM), 1.0 / HEAD_DIM, F32))[0].astype(BF16)
    inproj = dict(
        g=row(prm["mix_norm_g"]), w_in=prm["w_in"].astype(BF16), sgn=row(prm["sgu_norm_g"]),
        gq=row(jnp.tile(prm["q_norm_g"], (1, ATT_HEADS))), gk=row(jnp.tile(prm["k_norm_g"], (1, ATT_HEADS))),
        hm=hm, gb=row(prm["gate_b"]))

    i = np.arange(SGU_BLOCK)
    tri = (i[None, :] // CHUNK) <= (i[:, None] // CHUNK)
    sw = jnp.where(jnp.asarray(tri), prm["sgu_w"], 0.0)
    swc = jnp.concatenate([sw[:, g] for g in range(SGU_GROUPS)], axis=2).astype(BF16)
    sbm = jnp.repeat(jnp.swapaxes(prm["sgu_b"], 1, 2), SGU_GROUP_DIM, axis=2).astype(F32)
    invf, invr = _pool_inverse_counts()
    mixer = dict(invf=invf, invr=invr, bd=_block_diag(prm["pool_w"]).astype(BF16), ps=row(prm["pool_scale"]),
                 swc=swc, sbm=sbm, bias=_attention_bias(prm["rel_bias"]),
                 wa=prm["w_branch_a"].astype(BF16), wb=prm["w_branch_b"].astype(BF16),
                 wc=prm["w_branch_c"].astype(BF16), wo=prm["w_out"].astype(BF16))

    pad = jnp.zeros((nl, SUBLANES - N_GROUPS, D_MODEL), F32)
    wr = jnp.concatenate([jnp.swapaxes(prm["w_group_router"], 1, 2), pad,
                          jnp.swapaxes(prm["w_expert_router"], 1, 2)], axis=1)
    wr_hi = wr.astype(BF16)
    wr_lo = (wr - wr_hi.astype(F32)).astype(BF16)
    br = jnp.concatenate([prm["b_group_router"], jnp.zeros((nl, SUBLANES - N_GROUPS), F32),
                          prm["b_expert_router"]], axis=1).reshape(nl, -1, 1).astype(F32)
    mixer.update(gf=row(prm["ffn_norm_g"]), wr=jnp.concatenate([wr_hi, wr_lo], axis=1), br=br)

    moe = dict(win=prm["w_expert_in"].astype(BF16),
               wout=prm["w_expert_out"].astype(BF16))

    ple = dict(g=row(prm["ple_norm_g"]), wpi=prm["w_ple_in"].astype(BF16), wg=prm["w_ple_gate"].astype(BF16))
    return inproj, mixer, moe, ple


def _sort_by_group(gid, chunk):
    tr = MOE_TILE
    gid = gid.reshape(-1, chunk)
    order = jnp.argsort(gid, axis=1, stable=True).astype(I32)
    groups = jnp.arange(N_GROUPS, dtype=I32)
    cnts = jnp.sum(gid[:, :, None] == groups, axis=1).astype(I32)
    offs = jnp.cumsum(cnts, axis=1) - cnts
    ntiles = (cnts + (tr - 1)) // tr
    poffs = (jnp.cumsum(ntiles, axis=1) - ntiles) * tr
    slot = jnp.arange(chunk + N_GROUPS * tr, dtype=I32)[None, :, None]
    seg = jnp.sum(slot >= poffs[:, None, 1:], axis=2)
    pick = lambda v: jnp.sum(jnp.where(seg[:, :, None] == groups, v[:, None, :], 0), axis=2)
    src = pick(offs) + jnp.clip(slot[:, :, 0] - pick(poffs), 0, jnp.maximum(pick(cnts) - 1, 0))
    spad = jnp.take_along_axis(order, jnp.minimum(src, chunk - 1), axis=1) * SUBLANES
    return spad.reshape(-1), poffs.reshape(-1).astype(I32), ntiles.reshape(-1).astype(I32)


def kernel(x, p, mix_norm_g, w_in, pool_w, pool_scale, sgu_norm_g, sgu_w, sgu_b, q_norm_g, k_norm_g, rel_bias, gate_b, w_branch_a, w_branch_b, w_branch_c, w_out, ffn_norm_g, w_group_router, b_group_router, w_expert_router, b_expert_router, w_expert_in, w_expert_out, ple_norm_g, w_ple_in, w_ple_gate):
    prm = dict(mix_norm_g=mix_norm_g, w_in=w_in, pool_w=pool_w, pool_scale=pool_scale, sgu_norm_g=sgu_norm_g,
               sgu_w=sgu_w, sgu_b=sgu_b, q_norm_g=q_norm_g, k_norm_g=k_norm_g, rel_bias=rel_bias, gate_b=gate_b,
               w_branch_a=w_branch_a, w_branch_b=w_branch_b, w_branch_c=w_branch_c, w_out=w_out,
               ffn_norm_g=ffn_norm_g, w_group_router=w_group_router, b_group_router=b_group_router,
               w_expert_router=w_expert_router, b_expert_router=b_expert_router, w_expert_in=w_expert_in,
               w_expert_out=w_expert_out, ple_norm_g=ple_norm_g, w_ple_in=w_ple_in, w_ple_gate=w_ple_gate)
    b, s, d = x.shape
    t = b * s
    depth = p.shape[0]
    chunk = min(MOE_CHUNK, t)
    assert d == D_MODEL and s % ROWS_MIX == 0 and t % ROWS_PROJ == 0 and t % chunk == 0
    assert ROWS_MIX % ROWS_ATT == 0 and ROWS_MIX // ROWS_ATT >= HALO_BLOCKS
    c_in, c_mix, c_moe, c_ple = _prepare(prm)
    p3d = p.reshape(depth, t, PLE_DIM)
    xf = x.reshape(t, d)
    for l in range(depth):
        xf, hp, gid = _front(l, xf, c_in, c_mix, s)
        spad, poffs, ntiles = _sort_by_group(gid, chunk)
        mo = _moe(l, hp, spad, poffs, ntiles, c_moe, chunk)
        xf = _ple(l, xf, mo, p3d, c_ple)
    return xf.reshape(b, s, d)
```

```python
import functools

import jax
import jax.numpy as jnp
import numpy as np
from jax import lax
from jax.experimental import pallas as pl
from jax.experimental.pallas import tpu as pltpu

F32 = jnp.float32
BF16 = jnp.bfloat16
U32 = jnp.uint32
I32 = jnp.int32

D_MODEL = 1024
CHUNK = 64
PLE_DIM = 256
EPS = 1e-6
POOL_WINDOWS = (2, 4, 8, 16)
POOL_WIDTH = 256
POOL_GROUP_DIM = 64
SGU_BLOCK = 128
SGU_WIDTH = 256
SGU_GROUPS = 4
SGU_GROUP_DIM = 64
HEAD_DIM = 64
ATT_WIDTH = 512
ATT_HEADS = 8
LEFT_CHUNKS = 8
REL_CLIP = 128
N_BRANCH = 3
N_GROUPS = 4
EXPERTS_PER_GROUP = 8
D_EXPERT = 128
GROUP_FF = EXPERTS_PER_GROUP * D_EXPERT

C_A, C_U, C_V, C_Q, C_K, C_VV, C_G = 0, 256, 512, 768, 1280, 1792, 2304

LANES = 128
SUBLANES = 8
MXU_DIM = 256
VMEM_LIMIT = 56 * 1024 * 1024

ROWS_PROJ = 1024
ROWS_MIX = 512
ROWS_ATT = 256
HALO_BLOCKS = 2
POOL_HALO = 16
MOE_CHUNK = 2048
MOE_TILE = 256
MOE_STRIDE = MOE_TILE + 1
MOE_UNROLL = 32
PACK_W = D_MODEL // 2
PACK_ROWS = PACK_W // LANES
NEG = -1e30
GATE_SEG = 512
N_GATE_SEGS = N_BRANCH * D_MODEL // GATE_SEG
LOG2E = float(np.log2(np.e))


def _const_spec(shape):
    nd = len(shape)
    return pl.BlockSpec(shape, lambda *_: (0,) * nd, pipeline_mode=pl.Buffered(1))


def _layer_spec(arr, l):
    nd = arr.ndim - 1
    return pl.BlockSpec((None,) + arr.shape[1:], lambda *_: (l,) + (0,) * nd, pipeline_mode=pl.Buffered(1))


def _params(n_axes):
    return pltpu.CompilerParams(dimension_semantics=("arbitrary",) * n_axes,
                                vmem_limit_bytes=VMEM_LIMIT)


def _rms(x, g):
    ms = jnp.mean(x * x, axis=-1, keepdims=True)
    return x * lax.rsqrt(ms + EPS) * g


def _gelu_tanh(x):
    c = np.float32(np.sqrt(2.0 / np.pi))
    return 0.5 * x * (1.0 + jnp.tanh(c * (x + np.float32(0.044715) * (x * x * x))))


def _sigmoid(x):
    return 0.5 * jnp.tanh(0.5 * x) + 0.5


def _dot(a, b):
    return jnp.dot(a, b, preferred_element_type=F32)


def _dot_nt(a, b):
    return lax.dot_general(a, b, (((1,), (1,)), ((), ())), preferred_element_type=F32)


def _pack_pairs(x):
    w = x.shape[1] // 2
    bits = pltpu.bitcast(x.astype(BF16).astype(F32), U32)
    return (bits[:, :w] >> 16) | (bits[:, w:] & np.uint32(0xFFFF0000))


def _unpack_pairs(p):
    lo = pltpu.bitcast(p << 16, F32)
    hi = pltpu.bitcast(p & np.uint32(0xFFFF0000), F32)
    return jnp.concatenate([lo, hi], axis=1)


def _slab_rows(j, n):
    return pl.ds(j, n, stride=SUBLANES)


def _inproj_parts(x_ref, g_ref, w_ref, sgn_ref, gq_ref, gk_ref, hm_ref, gb_ref):
    h = _rms(x_ref[...], g_ref[...]).astype(BF16)

    def seg(c0, c1):
        return _dot(h, w_ref[:, c0:c1])

    def head_norm(z, g):
        zz = (z * z).astype(BF16)
        w = hm_ref.shape[0]
        ms = jnp.concatenate([_dot(zz[:, c0:c0 + w], hm_ref[...]) for c0 in range(0, ATT_WIDTH, w)], axis=1)
        return z * lax.rsqrt(ms + EPS) * g

    def branches(a_ref, u_ref, v_ref, q_ref, k_ref, vv_ref):
        a_ref[...] = seg(C_A, C_U).astype(BF16)
        u_ref[...] = _gelu_tanh(seg(C_U, C_V)).astype(BF16)
        v_ref[...] = _rms(_gelu_tanh(seg(C_V, C_Q)), sgn_ref[...]).astype(BF16)
        q_ref[...] = (head_norm(seg(C_Q, C_K), gq_ref[...]) * np.float32(HEAD_DIM ** -0.5 * LOG2E)).astype(BF16)
        k_ref[...] = head_norm(seg(C_K, C_VV), gk_ref[...]).astype(BF16)
        vv_ref[...] = seg(C_VV, C_G).astype(BF16)

    def gate_segment(i, gates_ref):
        c0 = i * GATE_SEG
        z = seg(C_G + c0, C_G + c0 + GATE_SEG) + gb_ref[:, c0:c0 + GATE_SEG]
        gates_ref[:, c0:c0 + GATE_SEG] = _sigmoid(z).astype(BF16)

    return branches, gate_segment


def _fill_attention_bias(boff_ref, bias_ref):
    ra, nk = bias_ref.shape[1], bias_ref.shape[2]
    qc = lax.broadcasted_iota(I32, (ra, nk), 0) // CHUNK
    kc = lax.broadcasted_iota(I32, (ra, nk), 1) // CHUNK
    valid = jnp.logical_and(kc >= qc, kc <= qc + LEFT_CHUNKS)
    for hd in range(bias_ref.shape[0]):
        row = jnp.broadcast_to(boff_ref[hd:hd + 1, :], (ra, boff_ref.shape[1]))
        band = pltpu.roll(row, 0, 1, stride=1, stride_axis=0)
        bias_ref[hd] = jnp.where(valid, band[:, 0:nk], NEG)


def _mixer_kernel(blocks_per_seq,
                  x_ref, a_ref, ap_ref, u_ref, v_ref, q_ref,
                  kp_ref, kc_ref, vp_ref, vc_ref, gates_ref,
                  invf_ref, invr_ref, bd_ref, ps_ref, swc_ref, sb_ref, boff_ref,
                  wa_ref, wb_ref, wc_ref, wo_ref, gf_ref, wr_ref, br_ref,
                  o_ref, hp_ref, gid_ref, pool_buf, x1_buf, bias_ref, between=None):
    r = ROWS_MIX
    step = pl.program_id(0)
    bi = step % blocks_per_seq
    first = bi == 0

    @pl.when(step == 0)
    def _():
        x1_buf[...] = jnp.zeros_like(x1_buf)
        _fill_attention_bias(boff_ref, bias_ref)

    a = a_ref[...].astype(F32)
    halo = jnp.where(first, 0.0, ap_ref[r - POOL_HALO:, :].astype(F32))
    pool_buf[0:POOL_HALO, :] = jnp.zeros((POOL_HALO, POOL_WIDTH), F32)
    pool_buf[POOL_HALO:2 * POOL_HALO, :] = halo
    pool_buf[2 * POOL_HALO:, :] = a
    lane_a = lax.broadcasted_iota(I32, (r, POOL_WIDTH), 1) // POOL_GROUP_DIM
    wsum = None
    for gi, k in enumerate((1, 2, 4, 8)):
        cur = pool_buf[POOL_HALO:, :] + pool_buf[POOL_HALO - k:r + 2 * POOL_HALO - k, :]
        pool_buf[POOL_HALO:, :] = cur
        level = cur[POOL_HALO:, :]
        wsum = level if wsum is None else jnp.where(lane_a >= gi, level, wsum)
    inv = jnp.where(first, invf_ref[...], invr_ref[...])
    pooled = (wsum * inv - a).astype(BF16)
    ya = (_dot(pooled, bd_ref[...]) * ps_ref[...]).astype(BF16)

    vb = v_ref[...]
    lane_b = lax.broadcasted_iota(I32, (SGU_BLOCK, SGU_WIDTH), 1) // SGU_GROUP_DIM
    zero_b = jnp.zeros((SGU_BLOCK, SGU_WIDTH), BF16)
    mixed = []
    for s in range(r // SGU_BLOCK):
        blk = vb[s * SGU_BLOCK:(s + 1) * SGU_BLOCK, :]
        rhs = jnp.concatenate([jnp.where(lane_b == g, blk, zero_b) for g in range(SGU_GROUPS)], axis=0)
        mixed.append(_dot(swc_ref[...], rhs) + sb_ref[...])
    yb = (u_ref[...].astype(F32) * jnp.concatenate(mixed, axis=0)).astype(BF16)

    ra = ROWS_ATT
    nsub = r // ra
    nblk = HALO_BLOCKS + 1
    lane_c = lax.broadcasted_iota(I32, (ra, LANES), 1)
    zero_q = jnp.zeros((ra, LANES), BF16)
    yc = []
    for sub in range(nsub):
        lo = (nsub + sub - HALO_BLOCKS) * ra
        sub_idx = bi * nsub + sub
        blk_ok = [sub_idx >= HALO_BLOCKS - j for j in range(HALO_BLOCKS)] + [None]
        ycs = []
        for pr in range(ATT_HEADS // 2):
            cols = slice(pr * LANES, (pr + 1) * LANES)
            qp = q_ref[sub * ra:(sub + 1) * ra, cols]
            qs = jnp.concatenate([jnp.where(lane_c < HEAD_DIM, qp, zero_q),
                                  jnp.where(lane_c >= HEAD_DIM, qp, zero_q)], axis=0)
            kcat = jnp.concatenate([kp_ref[lo:, cols], kc_ref[0:(sub + 1) * ra, cols]], axis=0)
            vcat = jnp.concatenate([vp_ref[lo:, cols], vc_ref[0:(sub + 1) * ra, cols]], axis=0)
            bias = []
            for j in range(nblk):
                bj = bias_ref[2 * pr:2 * pr + 2, :, j * ra:(j + 1) * ra].reshape(2 * ra, ra)
                bias.append(bj if blk_ok[j] is None else jnp.where(blk_ok[j], bj, NEG))
            sc = _dot_nt(qs, kcat) + jnp.concatenate(bias, axis=1)
            m = jnp.max(sc, axis=-1, keepdims=True)
            p = jnp.exp2(sc - m)
            denom = jnp.sum(p, axis=-1, keepdims=True)
            o = _dot(p.astype(BF16), vcat) / denom
            ycs.append(jnp.where(lane_c < HEAD_DIM, o[0:ra], o[ra:2 * ra]))
            if between is not None:
                between(sub * (ATT_HEADS // 2) + pr)
        yc.append(jnp.concatenate(ycs, axis=1))
    yc = jnp.concatenate(yc, axis=0).astype(BF16)

    _route(x1_buf[...], gf_ref, wr_ref, br_ref, hp_ref, gid_ref)

    d = D_MODEL
    merged = (gates_ref[:, 0:d].astype(F32) * _dot(ya, wa_ref[...])
              + gates_ref[:, d:2 * d].astype(F32) * _dot(yb, wb_ref[...])
              + gates_ref[:, 2 * d:3 * d].astype(F32) * _dot(yc, wc_ref[...]))
    x1 = x_ref[...] + _dot(merged.astype(BF16), wo_ref[...])
    o_ref[...] = x1
    x1_buf[...] = x1


N_INPROJ_REFS = 7


def _front_kernel(blocks_per_seq, x_ref, *refs):
    proj_refs, refs = refs[:N_INPROJ_REFS], refs[N_INPROJ_REFS:]
    (a_buf, u_buf, v_buf, q_buf, k_buf, vv_buf, gates_buf, pool_buf, x1_buf, bias_buf) = refs[-10:]
    mix_refs, out_refs = refs[:-13], refs[-13:-10]
    r = ROWS_MIX
    step = pl.program_id(0)
    prev, cur = pl.ds(0, r), pl.ds(r, r)
    carried = (a_buf, k_buf, vv_buf)

    @pl.when(step == 0)
    def _():
        for buf in carried:
            buf[...] = jnp.zeros_like(buf)

    live = step < pl.num_programs(0) - 1

    @pl.when(live)
    def _():
        for buf in carried:
            buf[prev, :] = buf[cur, :]
        branches, gate_segment = _inproj_parts(x_ref, *proj_refs)
        branches(a_buf.at[cur], u_buf, v_buf, q_buf, k_buf.at[cur], vv_buf.at[cur])

        def between(unit):
            if unit < N_GATE_SEGS:
                gate_segment(unit, gates_buf)

        _mixer_kernel(blocks_per_seq, x_ref, a_buf.at[cur], a_buf.at[prev], u_buf, v_buf, q_buf,
                      k_buf.at[prev], k_buf.at[cur], vv_buf.at[prev], vv_buf.at[cur], gates_buf,
                      *mix_refs, *out_refs, pool_buf, x1_buf, bias_buf, between=between)

    @pl.when(jnp.logical_not(live))
    def _():
        _route(x1_buf[...], *mix_refs[-3:], *out_refs[1:])


def _front(l, x2d, ci, c, seq):
    t = x2d.shape[0]
    r = ROWS_MIX
    nb = t // r
    row = lambda w: pl.BlockSpec((r, w), lambda i: (jnp.minimum(i, nb - 1), 0))
    routed = lambda shape: pl.BlockSpec(shape, lambda i: (jnp.maximum(i - 1, 0),) + (0,) * (len(shape) - 1))
    proj = [ci["g"], ci["w_in"], ci["sgn"], ci["gq"], ci["gk"]]
    layer = [c["bd"], c["ps"], c["swc"], c["sbm"], c["bias"], c["wa"], c["wb"], c["wc"], c["wo"],
             c["gf"], c["wr"], c["br"]]
    in_specs = ([row(D_MODEL)] + [_layer_spec(w, l) for w in proj]
                + [_const_spec(ci["hm"].shape), _layer_spec(ci["gb"], l),
                   _const_spec(c["invf"].shape), _const_spec(c["invr"].shape)] + [_layer_spec(w, l) for w in layer])
    vmem = lambda rows, w, dt: pltpu.VMEM((rows, w), dt)
    return pl.pallas_call(
        functools.partial(_front_kernel, seq // r),
        grid=(nb + 1,),
        in_specs=in_specs,
        out_specs=[row(D_MODEL), routed((r * SUBLANES, LANES)), routed((1, 1, r))],
        out_shape=[jax.ShapeDtypeStruct((t, D_MODEL), F32), jax.ShapeDtypeStruct((t * SUBLANES, LANES), U32),
                   jax.ShapeDtypeStruct((nb, 1, r), I32)],
        scratch_shapes=[vmem(2 * r, POOL_WIDTH, BF16), vmem(r, SGU_WIDTH, BF16), vmem(r, SGU_WIDTH, BF16),
                        vmem(r, ATT_WIDTH, BF16), vmem(2 * r, ATT_WIDTH, BF16), vmem(2 * r, ATT_WIDTH, BF16),
                        vmem(r, N_BRANCH * D_MODEL, BF16),
                        vmem(r + 2 * POOL_HALO, POOL_WIDTH, F32), vmem(r, D_MODEL, F32),
                        pltpu.VMEM((ATT_HEADS, ROWS_ATT, (HALO_BLOCKS + 1) * ROWS_ATT), F32)],
        compiler_params=_params(1),
        name="front",
    )(x2d, *proj, ci["hm"], ci["gb"], c["invf"], c["invr"], *layer)


ROUTER_ROWS = SUBLANES + N_GROUPS * EXPERTS_PER_GROUP


def _route(x, g_ref, wr_ref, br_ref, hp_ref, gid_ref):
    tm = x.shape[0]
    h = _rms(x, g_ref[...])
    h_hi = h.astype(BF16)
    h_lo = (h - h_hi.astype(F32)).astype(BF16)
    nr = ROUTER_ROWS
    l1 = _dot_nt(wr_ref[...], h_hi)
    l2 = _dot_nt(wr_ref[0:nr, :], h_lo)
    logits = l1[0:nr] + l1[nr:2 * nr] + l2 + br_ref[...]

    rows = lax.broadcasted_iota(I32, (SUBLANES, tm), 0).astype(F32)
    big = np.float32(1 << 20)
    gl = jnp.where(rows < N_GROUPS, logits[0:SUBLANES], NEG)
    gmax = jnp.max(gl, axis=0, keepdims=True)
    g_p = 1.0 / jnp.sum(jnp.exp(gl - gmax), axis=0, keepdims=True)
    g_idx = jnp.min(jnp.where(gl == gmax, rows, big), axis=0, keepdims=True)

    el = logits[SUBLANES:2 * SUBLANES]
    for gi in range(1, N_GROUPS):
        el = jnp.where(g_idx == gi, logits[SUBLANES * (gi + 1):SUBLANES * (gi + 2)], el)
    emax = jnp.max(el, axis=0, keepdims=True)
    esum = jnp.sum(jnp.exp(el - emax), axis=0, keepdims=True)
    i0 = jnp.min(jnp.where(el == emax, rows, big), axis=0, keepdims=True)
    el2 = jnp.where(rows == i0, -jnp.inf, el)
    m2 = jnp.max(el2, axis=0, keepdims=True)
    i1 = jnp.min(jnp.where(el2 == m2, rows, big), axis=0, keepdims=True)
    p0 = 1.0 / esum
    p1 = jnp.exp(m2 - emax) / esum
    w0 = g_p * p0 / (p0 + p1)
    w1 = g_p * p1 / (p0 + p1)
    gates = jnp.where(rows == i0, w0, 0.0) + jnp.where(rows == i1, w1, 0.0)

    gates_t = jnp.concatenate([gates, jnp.zeros((LANES - SUBLANES, tm), F32)], axis=0).T
    packed = _pack_pairs(h)
    for j in range(PACK_ROWS):
        hp_ref[_slab_rows(j, tm), :] = packed[:, j * LANES:(j + 1) * LANES]
    hp_ref[_slab_rows(PACK_ROWS, tm), :] = pltpu.bitcast(gates_t, U32)
    for j in range(PACK_ROWS + 1, SUBLANES):
        hp_ref[_slab_rows(j, tm), :] = jnp.zeros((tm, LANES), U32)
    gid_ref[...] = g_idx.astype(I32).reshape(1, 1, tm)


def _moe_kernel(padded, spad_ref, poffs_ref, ntiles_ref, hp_ref, win_ref, wout_ref, o_ref, xt_ref, yt_ref):
    c = pl.program_id(0)
    g = pl.program_id(1)
    seg0 = c * padded + poffs_ref[c * N_GROUPS + g]
    tr = MOE_TILE
    st = MOE_STRIDE

    def tile_body(t, carry):
        r0 = seg0 + t * tr

        def slab(entry):
            return pl.ds(pl.multiple_of(spad_ref[entry], SUBLANES), SUBLANES)

        def gather(i, cr):
            rr, entry = i * MOE_UNROLL, r0 + i * MOE_UNROLL
            for uu in range(MOE_UNROLL):
                xt_ref[pl.ds(rr + uu, SUBLANES, stride=st), :] = hp_ref[slab(entry + uu), :]
            return cr

        lax.fori_loop(0, tr // MOE_UNROLL, gather, 0)

        packed = jnp.concatenate([xt_ref[pl.ds(j * st, tr), :] for j in range(PACK_ROWS)], axis=1)
        xb = _unpack_pairs(packed).astype(BF16)
        gates = pltpu.bitcast(xt_ref[pl.ds(PACK_ROWS * st, tr), :], F32)
        acts = []
        for e in range(EXPERTS_PER_GROUP):
            hu = _dot(xb, win_ref[e])
            hg = hu[:, 0:D_EXPERT]
            acts.append((hg * _sigmoid(hg) * hu[:, D_EXPERT:] * gates[:, e:e + 1]).astype(BF16))
        y = _dot(jnp.concatenate(acts, axis=1), wout_ref[...].reshape(GROUP_FF, D_MODEL))
        for j in range(SUBLANES):
            yt_ref[pl.ds(j * st, tr), :] = y[:, j * LANES:(j + 1) * LANES]

        def scatter(i, cr):
            rr, entry = i * MOE_UNROLL, r0 + i * MOE_UNROLL
            for uu in range(MOE_UNROLL):
                o_ref[slab(entry + uu), :] = yt_ref[pl.ds(rr + uu, SUBLANES, stride=st), :]
            return cr

        lax.fori_loop(0, tr // MOE_UNROLL, scatter, 0)
        return carry

    lax.fori_loop(0, ntiles_ref[c * N_GROUPS + g], tile_body, 0)


def _moe(l, hp, spad, poffs, ntiles, c, chunk):
    t = hp.shape[0] // SUBLANES
    win, wout = c["win"], c["wout"]
    grid_spec = pltpu.PrefetchScalarGridSpec(
        num_scalar_prefetch=3,
        grid=(t // chunk, N_GROUPS),
        in_specs=[pl.BlockSpec((chunk * SUBLANES, LANES), lambda ci, g, *_: (ci, 0)),
                  pl.BlockSpec((None, EXPERTS_PER_GROUP) + win.shape[2:], lambda ci, g, *_: (l, g, 0, 0)),
                  pl.BlockSpec((None, EXPERTS_PER_GROUP) + wout.shape[2:], lambda ci, g, *_: (l, g, 0, 0))],
        out_specs=pl.BlockSpec((chunk * SUBLANES, LANES), lambda ci, g, *_: (ci, 0)),
        scratch_shapes=[pltpu.VMEM((MOE_STRIDE * SUBLANES, LANES), U32),
                        pltpu.VMEM((MOE_STRIDE * SUBLANES, LANES), F32)],
    )
    return pl.pallas_call(
        functools.partial(_moe_kernel, spad.shape[0] // (t // chunk)),
        grid_spec=grid_spec,
        out_shape=jax.ShapeDtypeStruct((t * SUBLANES, LANES), F32),
        compiler_params=_params(2),
        name="moe",
    )(spad, poffs, ntiles, hp, win, wout)


def _ple_kernel(x_ref, mo_ref, p_ref, g_ref, wpi_ref, wg_ref, o_ref):
    tm = x_ref.shape[0]
    mo = jnp.concatenate([mo_ref[_slab_rows(j, tm), :] for j in range(SUBLANES)], axis=1)
    x = x_ref[...] + mo
    hn = _rms(x, g_ref[...]).astype(BF16)
    gate = _sigmoid(_dot(hn, wg_ref[...]))
    emb = _dot(p_ref[...].astype(BF16), wpi_ref[...])
    o_ref[...] = x + emb * gate


def _ple(l, x2d, mo, p3d, c):
    t = x2d.shape[0]
    tm = ROWS_PROJ
    row = lambda w: pl.BlockSpec((tm, w), lambda i: (i, 0))
    layer = [c["g"], c["wpi"], c["wg"]]
    return pl.pallas_call(
        _ple_kernel,
        grid=(t // tm,),
        in_specs=[row(D_MODEL), pl.BlockSpec((tm * SUBLANES, LANES), lambda i: (i, 0)),
                  pl.BlockSpec((None, tm, PLE_DIM), lambda i: (l, i, 0))] + [_layer_spec(w, l) for w in layer],
        out_specs=row(D_MODEL),
        out_shape=jax.ShapeDtypeStruct((t, D_MODEL), F32),
        compiler_params=_params(1),
        name="ple",
    )(x2d, mo, p3d, *layer)


def _bias_per_offset(rel_bias):
    r = ROWS_ATT
    nk = (HALO_BLOCKS + 1) * r
    p = nk + r
    off = np.arange(p)
    off = np.where(off < nk, off, off - p)
    rel = np.clip(HALO_BLOCKS * r - off, -REL_CLIP, REL_CLIP) + REL_CLIP
    return rel_bias[..., rel].astype(F32) * np.float32(LOG2E)


def _pool_inverse_counts():
    r = ROWS_MIX
    win = np.repeat(np.asarray(POOL_WINDOWS, np.float32), POOL_GROUP_DIM)[None, :]
    pos = np.arange(1, r + 1, dtype=np.float32)[:, None]
    return jnp.asarray(1.0 / np.minimum(pos, win), F32), jnp.asarray(1.0 / win, F32)


def _block_diag(w):
    nl, g, c, d = w.shape
    eye = jnp.eye(g, dtype=w.dtype)
    return jnp.einsum("gh,lgcd->lgchd", eye, w).reshape(nl, g * c, g * d)


def _prepare(prm):
    nl = prm["w_in"].shape[0]
    row = lambda v: v.reshape(nl, 1, -1).astype(F32)
    hm = _block_diag(jnp.full((1, MXU_DIM // HEAD_DIM, HEAD_DIM, HEAD_DIM), 1.0 / HEAD_DIM, F32))[0].astype(BF16)
    inproj = dict(
        g=row(prm["mix_norm_g"]), w_in=prm["w_in"].astype(BF16), sgn=row(prm["sgu_norm_g"]),
        gq=row(jnp.tile(prm["q_norm_g"], (1, ATT_HEADS))), gk=row(jnp.tile(prm["k_norm_g"], (1, ATT_HEADS))),
        hm=hm, gb=row(prm["gate_b"]))

    i = np.arange(SGU_BLOCK)
    tri = (i[None, :] // CHUNK) <= (i[:, None] // CHUNK)
    sw = jnp.where(jnp.asarray(tri), prm["sgu_w"], 0.0)
    swc = jnp.concatenate([sw[:, g] for g in range(SGU_GROUPS)], axis=2).astype(BF16)
    sbm = jnp.repeat(jnp.swapaxes(prm["sgu_b"], 1, 2), SGU_GROUP_DIM, axis=2).astype(F32)
    invf, invr = _pool_inverse_counts()
    mixer = dict(invf=invf, invr=invr, bd=_block_diag(prm["pool_w"]).astype(BF16), ps=row(prm["pool_scale"]),
                 swc=swc, sbm=sbm, bias=_bias_per_offset(prm["rel_bias"]),
                 wa=prm["w_branch_a"].astype(BF16), wb=prm["w_branch_b"].astype(BF16),
                 wc=prm["w_branch_c"].astype(BF16), wo=prm["w_out"].astype(BF16))

    pad = jnp.zeros((nl, SUBLANES - N_GROUPS, D_MODEL), F32)
    wr = jnp.concatenate([jnp.swapaxes(prm["w_group_router"], 1, 2), pad,
                          jnp.swapaxes(prm["w_expert_router"], 1, 2)], axis=1)
    wr_hi = wr.astype(BF16)
    wr_lo = (wr - wr_hi.astype(F32)).astype(BF16)
    br = jnp.concatenate([prm["b_group_router"], jnp.zeros((nl, SUBLANES - N_GROUPS), F32),
                          prm["b_expert_router"]], axis=1).reshape(nl, -1, 1).astype(F32)
    mixer.update(gf=row(prm["ffn_norm_g"]), wr=jnp.concatenate([wr_hi, wr_lo], axis=1), br=br)

    moe = dict(win=prm["w_expert_in"].astype(BF16),
               wout=prm["w_expert_out"].astype(BF16))

    ple = dict(g=row(prm["ple_norm_g"]), wpi=prm["w_ple_in"].astype(BF16), wg=prm["w_ple_gate"].astype(BF16))
    return inproj, mixer, moe, ple


def _sort_by_group(gid, chunk):
    tr = MOE_TILE
    gid = gid.reshape(-1, chunk)
    order = jnp.argsort(gid, axis=1, stable=True).astype(I32)
    groups = jnp.arange(N_GROUPS, dtype=I32)
    cnts = jnp.sum(gid[:, :, None] == groups, axis=1).astype(I32)
    offs = jnp.cumsum(cnts, axis=1) - cnts
    ntiles = (cnts + (tr - 1)) // tr
    poffs = (jnp.cumsum(ntiles, axis=1) - ntiles) * tr
    slot = jnp.arange(chunk + N_GROUPS * tr, dtype=I32)[None, :, None]
    seg = jnp.sum(slot >= poffs[:, None, 1:], axis=2)
    pick = lambda v: jnp.sum(jnp.where(seg[:, :, None] == groups, v[:, None, :], 0), axis=2)
    src = pick(offs) + jnp.clip(slot[:, :, 0] - pick(poffs), 0, jnp.maximum(pick(cnts) - 1, 0))
    spad = jnp.take_along_axis(order, jnp.minimum(src, chunk - 1), axis=1) * SUBLANES
    return spad.reshape(-1), poffs.reshape(-1).astype(I32), ntiles.reshape(-1).astype(I32)


def kernel(x, p, mix_norm_g, w_in, pool_w, pool_scale, sgu_norm_g, sgu_w, sgu_b, q_norm_g, k_norm_g, rel_bias, gate_b, w_branch_a, w_branch_b, w_branch_c, w_out, ffn_norm_g, w_group_router, b_group_router, w_expert_router, b_expert_router, w_expert_in, w_expert_out, ple_norm_g, w_ple_in, w_ple_gate):
    prm = dict(mix_norm_g=mix_norm_g, w_in=w_in, pool_w=pool_w, pool_scale=pool_scale, sgu_norm_g=sgu_norm_g,
               sgu_w=sgu_w, sgu_b=sgu_b, q_norm_g=q_norm_g, k_norm_g=k_norm_g, rel_bias=rel_bias, gate_b=gate_b,
               w_branch_a=w_branch_a, w_branch_b=w_branch_b, w_branch_c=w_branch_c, w_out=w_out,
               ffn_norm_g=ffn_norm_g, w_group_router=w_group_router, b_group_router=b_group_router,
               w_expert_router=w_expert_router, b_expert_router=b_expert_router, w_expert_in=w_expert_in,
               w_expert_out=w_expert_out, ple_norm_g=ple_norm_g, w_ple_in=w_ple_in, w_ple_gate=w_ple_gate)
    b, s, d = x.shape
    t = b * s
    depth = p.shape[0]
    chunk = min(MOE_CHUNK, t)
    assert d == D_MODEL and s % ROWS_MIX == 0 and t % ROWS_PROJ == 0 and t % chunk == 0
    assert ROWS_MIX % ROWS_ATT == 0 and ROWS_MIX // ROWS_ATT >= HALO_BLOCKS
    c_in, c_mix, c_moe, c_ple = _prepare(prm)
    p3d = p.reshape(depth, t, PLE_DIM)
    xf = x.reshape(t, d)
    for l in range(depth):
        xf, hp, gid = _front(l, xf, c_in, c_mix, s)
        spad, poffs, ntiles = _sort_by_group(gid, chunk)
        mo = _moe(l, hp, spad, poffs, ntiles, c_moe, chunk)
        xf = _ple(l, xf, mo, p3d, c_ple)
    return xf.reshape(b, s, d)
```

```python
import functools

import jax
import jax.numpy as jnp
import numpy as np
from jax import lax
from jax.experimental import pallas as pl
from jax.experimental.pallas import tpu as pltpu

F32 = jnp.float32
BF16 = jnp.bfloat16
U32 = jnp.uint32
I32 = jnp.int32

D_MODEL = 1024
CHUNK = 64
PLE_DIM = 256
EPS = 1e-6
POOL_WINDOWS = (2, 4, 8, 16)
POOL_WIDTH = 256
POOL_GROUP_DIM = 64
SGU_BLOCK = 128
SGU_WIDTH = 256
SGU_GROUPS = 4
SGU_GROUP_DIM = 64
HEAD_DIM = 64
ATT_WIDTH = 512
ATT_HEADS = 8
LEFT_CHUNKS = 8
REL_CLIP = 128
N_BRANCH = 3
N_GROUPS = 4
EXPERTS_PER_GROUP = 8
D_EXPERT = 128
GROUP_FF = EXPERTS_PER_GROUP * D_EXPERT

C_A, C_U, C_V, C_Q, C_K, C_VV, C_G = 0, 256, 512, 768, 1280, 1792, 2304

LANES = 128
SUBLANES = 8
MXU_DIM = 256
VMEM_LIMIT = 60 * 1024 * 1024

ROWS_PROJ = 1024
ROWS_MIX = 512
ROWS_ATT = 256
HALO_BLOCKS = 2
POOL_HALO = 16
MOE_CHUNK = 2048
MOE_TILE = 256
MOE_STRIDE = MOE_TILE + 1
MOE_UNROLL = 32
PACK_W = D_MODEL // 2
PACK_ROWS = PACK_W // LANES
NEG = -1e30
GATE_SEG = 512
N_GATE_SEGS = N_BRANCH * D_MODEL // GATE_SEG
LOG2E = float(np.log2(np.e))


def _const_spec(shape):
    nd = len(shape)
    return pl.BlockSpec(shape, lambda *_: (0,) * nd, pipeline_mode=pl.Buffered(1))


def _layer_spec(arr, l):
    nd = arr.ndim - 1
    return pl.BlockSpec((None,) + arr.shape[1:], lambda *_: (l,) + (0,) * nd, pipeline_mode=pl.Buffered(1))


def _params(n_axes):
    return pltpu.CompilerParams(dimension_semantics=("arbitrary",) * n_axes,
                                vmem_limit_bytes=VMEM_LIMIT)


def _rms(x, g):
    ms = jnp.mean(x * x, axis=-1, keepdims=True)
    return x * lax.rsqrt(ms + EPS) * g


def _gelu_tanh(x):
    c = np.float32(np.sqrt(2.0 / np.pi))
    return 0.5 * x * (1.0 + jnp.tanh(c * (x + np.float32(0.044715) * (x * x * x))))


def _sigmoid(x):
    return 0.5 * jnp.tanh(0.5 * x) + 0.5


def _dot(a, b):
    return jnp.dot(a, b, preferred_element_type=F32)


def _dot_nt(a, b):
    return lax.dot_general(a, b, (((1,), (1,)), ((), ())), preferred_element_type=F32)


def _pack_pairs(x):
    w = x.shape[1] // 2
    bits = pltpu.bitcast(x.astype(BF16).astype(F32), U32)
    return (bits[:, :w] >> 16) | (bits[:, w:] & np.uint32(0xFFFF0000))


def _unpack_pairs(p):
    lo = pltpu.bitcast(p << 16, F32)
    hi = pltpu.bitcast(p & np.uint32(0xFFFF0000), F32)
    return jnp.concatenate([lo, hi], axis=1)


def _slab_rows(j, n):
    return pl.ds(j, n, stride=SUBLANES)


def _inproj_parts(x_ref, g_ref, w_ref, sgn_ref, gq_ref, gk_ref, hm_ref, gb_ref):
    h = _rms(x_ref[...], g_ref[...]).astype(BF16)

    def seg(c0, c1):
        return _dot(h, w_ref[:, c0:c1])

    def head_norm(z, g):
        zz = (z * z).astype(BF16)
        w = hm_ref.shape[0]
        ms = jnp.concatenate([_dot(zz[:, c0:c0 + w], hm_ref[...]) for c0 in range(0, ATT_WIDTH, w)], axis=1)
        return z * lax.rsqrt(ms + EPS) * g

    def side_branches(a_ref, u_ref, v_ref):
        a_ref[...] = seg(C_A, C_U).astype(BF16)
        u_ref[...] = _gelu_tanh(seg(C_U, C_V)).astype(BF16)
        v_ref[...] = _rms(_gelu_tanh(seg(C_V, C_Q)), sgn_ref[...]).astype(BF16)

    def attention_branches(q_ref, k_ref, vv_ref):
        q_ref[...] = (head_norm(seg(C_Q, C_K), gq_ref[...]) * np.float32(HEAD_DIM ** -0.5 * LOG2E)).astype(BF16)
        k_ref[...] = head_norm(seg(C_K, C_VV), gk_ref[...]).astype(BF16)
        vv_ref[...] = seg(C_VV, C_G).astype(BF16)

    def gate_segment(i, gates_ref):
        c0 = i * GATE_SEG
        z = seg(C_G + c0, C_G + c0 + GATE_SEG) + gb_ref[:, c0:c0 + GATE_SEG]
        gates_ref[:, c0:c0 + GATE_SEG] = _sigmoid(z).astype(BF16)

    return side_branches, attention_branches, gate_segment


def _fill_attention_bias(boff_ref, bias_ref):
    ra, nk = bias_ref.shape[1], bias_ref.shape[2]
    qc = lax.broadcasted_iota(I32, (ra, nk), 0) // CHUNK
    kc = lax.broadcasted_iota(I32, (ra, nk), 1) // CHUNK
    valid = jnp.logical_and(kc >= qc, kc <= qc + LEFT_CHUNKS)
    for hd in range(bias_ref.shape[0]):
        row = jnp.broadcast_to(boff_ref[hd:hd + 1, :], (ra, boff_ref.shape[1]))
        band = pltpu.roll(row, 0, 1, stride=1, stride_axis=0)
        bias_ref[hd] = jnp.where(valid, band[:, 0:nk], NEG)


def _mixer_kernel(blocks_per_seq,
                  x_ref, a_ref, ap_ref, u_ref, v_ref, q_ref,
                  kp_ref, kc_ref, vp_ref, vc_ref, gates_ref,
                  invf_ref, invr_ref, bd_ref, ps_ref, swc_ref, sb_ref, boff_ref,
                  wa_ref, wb_ref, wc_ref, wo_ref, gf_ref, wr_ref, br_ref,
                  o_ref, hp_ref, gid_ref, pool_buf, x1_buf, bias_ref, between=None):
    r = ROWS_MIX
    step = pl.program_id(0)
    bi = step % blocks_per_seq
    first = bi == 0

    @pl.when(step == 0)
    def _():
        x1_buf[...] = jnp.zeros_like(x1_buf)
        _fill_attention_bias(boff_ref, bias_ref)

    ra = ROWS_ATT
    nsub = r // ra
    nblk = HALO_BLOCKS + 1
    lane_c = lax.broadcasted_iota(I32, (ra, LANES), 1)
    zero_q = jnp.zeros((ra, LANES), BF16)
    yc = []
    for sub in range(nsub):
        lo = (nsub + sub - HALO_BLOCKS) * ra
        sub_idx = bi * nsub + sub
        blk_ok = [sub_idx >= HALO_BLOCKS - j for j in range(HALO_BLOCKS)] + [None]
        ycs = []
        for pr in range(ATT_HEADS // 2):
            cols = slice(pr * LANES, (pr + 1) * LANES)
            qp = q_ref[sub * ra:(sub + 1) * ra, cols]
            qs = jnp.concatenate([jnp.where(lane_c < HEAD_DIM, qp, zero_q),
                                  jnp.where(lane_c >= HEAD_DIM, qp, zero_q)], axis=0)
            kcat = jnp.concatenate([kp_ref[lo:, cols], kc_ref[0:(sub + 1) * ra, cols]], axis=0)
            vcat = jnp.concatenate([vp_ref[lo:, cols], vc_ref[0:(sub + 1) * ra, cols]], axis=0)
            bias = []
            for j in range(nblk):
                bj = bias_ref[2 * pr:2 * pr + 2, :, j * ra:(j + 1) * ra].reshape(2 * ra, ra)
                bias.append(bj if blk_ok[j] is None else jnp.where(blk_ok[j], bj, NEG))
            sc = _dot_nt(qs, kcat) + jnp.concatenate(bias, axis=1)
            m = jnp.max(sc, axis=-1, keepdims=True)
            p = jnp.exp2(sc - m)
            denom = jnp.sum(p, axis=-1, keepdims=True)
            o = _dot(p.astype(BF16), vcat) / denom
            ycs.append(jnp.where(lane_c < HEAD_DIM, o[0:ra], o[ra:2 * ra]))
            if between is not None:
                between(sub * (ATT_HEADS // 2) + pr)
        yc.append(jnp.concatenate(ycs, axis=1))
    yc = jnp.concatenate(yc, axis=0).astype(BF16)

    a = a_ref[...].astype(F32)
    halo = jnp.where(first, 0.0, ap_ref[r - POOL_HALO:, :].astype(F32))
    pool_buf[0:POOL_HALO, :] = jnp.zeros((POOL_HALO, POOL_WIDTH), F32)
    pool_buf[POOL_HALO:2 * POOL_HALO, :] = halo
    pool_buf[2 * POOL_HALO:, :] = a
    lane_a = lax.broadcasted_iota(I32, (r, POOL_WIDTH), 1) // POOL_GROUP_DIM
    wsum = None
    for gi, k in enumerate((1, 2, 4, 8)):
        cur = pool_buf[POOL_HALO:, :] + pool_buf[POOL_HALO - k:r + 2 * POOL_HALO - k, :]
        pool_buf[POOL_HALO:, :] = cur
        level = cur[POOL_HALO:, :]
        wsum = level if wsum is None else jnp.where(lane_a >= gi, level, wsum)
    inv = jnp.where(first, invf_ref[...], invr_ref[...])
    pooled = (wsum * inv - a).astype(BF16)
    ya = (_dot(pooled, bd_ref[...]) * ps_ref[...]).astype(BF16)

    vb = v_ref[...]
    lane_b = lax.broadcasted_iota(I32, (SGU_BLOCK, SGU_WIDTH), 1) // SGU_GROUP_DIM
    zero_b = jnp.zeros((SGU_BLOCK, SGU_WIDTH), BF16)
    mixed = []
    for s in range(r // SGU_BLOCK):
        blk = vb[s * SGU_BLOCK:(s + 1) * SGU_BLOCK, :]
        rhs = jnp.concatenate([jnp.where(lane_b == g, blk, zero_b) for g in range(SGU_GROUPS)], axis=0)
        mixed.append(_dot(swc_ref[...], rhs) + sb_ref[...])
    yb = (u_ref[...].astype(F32) * jnp.concatenate(mixed, axis=0)).astype(BF16)

    _route(x1_buf[...], gf_ref, wr_ref, br_ref, hp_ref, gid_ref)

    d = D_MODEL
    merged = (gates_ref[:, 0:d].astype(F32) * _dot(ya, wa_ref[...])
              + gates_ref[:, d:2 * d].astype(F32) * _dot(yb, wb_ref[...])
              + gates_ref[:, 2 * d:3 * d].astype(F32) * _dot(yc, wc_ref[...]))
    x1 = x_ref[...] + _dot(merged.astype(BF16), wo_ref[...])
    o_ref[...] = x1
    x1_buf[...] = x1


N_INPROJ_REFS = 7


def _front_kernel(blocks_per_seq, x_ref, *refs):
    proj_refs, refs = refs[:N_INPROJ_REFS], refs[N_INPROJ_REFS:]
    (a_buf, u_buf, v_buf, q_buf, k_buf, vv_buf, gates_buf, pool_buf, x1_buf, bias_buf) = refs[-10:]
    mix_refs, out_refs = refs[:-13], refs[-13:-10]
    r = ROWS_MIX
    step = pl.program_id(0)
    prev, cur = pl.ds(0, r), pl.ds(r, r)
    carried = (a_buf, k_buf, vv_buf)

    @pl.when(step == 0)
    def _():
        for buf in carried:
            buf[...] = jnp.zeros_like(buf)

    live = step < pl.num_programs(0) - 1

    @pl.when(live)
    def _():
        for buf in carried:
            buf[prev, :] = buf[cur, :]
        side_branches, attention_branches, gate_segment = _inproj_parts(x_ref, *proj_refs)
        attention_branches(q_buf, k_buf.at[cur], vv_buf.at[cur])

        def between(unit):
            if unit < N_GATE_SEGS:
                gate_segment(unit, gates_buf)
            elif unit == N_GATE_SEGS:
                side_branches(a_buf.at[cur], u_buf, v_buf)

        _mixer_kernel(blocks_per_seq, x_ref, a_buf.at[cur], a_buf.at[prev], u_buf, v_buf, q_buf,
                      k_buf.at[prev], k_buf.at[cur], vv_buf.at[prev], vv_buf.at[cur], gates_buf,
                      *mix_refs, *out_refs, pool_buf, x1_buf, bias_buf, between=between)

    @pl.when(jnp.logical_not(live))
    def _():
        _route(x1_buf[...], *mix_refs[-3:], *out_refs[1:])


def _front(l, x2d, ci, c, seq):
    t = x2d.shape[0]
    r = ROWS_MIX
    nb = t // r
    row = lambda w: pl.BlockSpec((r, w), lambda i: (jnp.minimum(i, nb - 1), 0))
    routed = lambda shape: pl.BlockSpec(shape, lambda i: (jnp.maximum(i - 1, 0),) + (0,) * (len(shape) - 1))
    proj = [ci["g"], ci["w_in"], ci["sgn"], ci["gq"], ci["gk"]]
    layer = [c["bd"], c["ps"], c["swc"], c["sbm"], c["bias"], c["wa"], c["wb"], c["wc"], c["wo"],
             c["gf"], c["wr"], c["br"]]
    in_specs = ([row(D_MODEL)] + [_layer_spec(w, l) for w in proj]
                + [_const_spec(ci["hm"].shape), _layer_spec(ci["gb"], l),
                   _const_spec(c["invf"].shape), _const_spec(c["invr"].shape)] + [_layer_spec(w, l) for w in layer])
    vmem = lambda rows, w, dt: pltpu.VMEM((rows, w), dt)
    return pl.pallas_call(
        functools.partial(_front_kernel, seq // r),
        grid=(nb + 1,),
        in_specs=in_specs,
        out_specs=[row(D_MODEL), routed((r * SUBLANES, LANES)), routed((1, 1, r))],
        out_shape=[jax.ShapeDtypeStruct((t, D_MODEL), F32), jax.ShapeDtypeStruct((t * SUBLANES, LANES), U32),
                   jax.ShapeDtypeStruct((nb, 1, r), I32)],
        scratch_shapes=[vmem(2 * r, POOL_WIDTH, BF16), vmem(r, SGU_WIDTH, BF16), vmem(r, SGU_WIDTH, BF16),
                        vmem(r, ATT_WIDTH, BF16), vmem(2 * r, ATT_WIDTH, BF16), vmem(2 * r, ATT_WIDTH, BF16),
                        vmem(r, N_BRANCH * D_MODEL, BF16),
                        vmem(r + 2 * POOL_HALO, POOL_WIDTH, F32), vmem(r, D_MODEL, F32),
                        pltpu.VMEM((ATT_HEADS, ROWS_ATT, (HALO_BLOCKS + 1) * ROWS_ATT), F32)],
        compiler_params=_params(1),
        name="front",
    )(x2d, *proj, ci["hm"], ci["gb"], c["invf"], c["invr"], *layer)


ROUTER_ROWS = SUBLANES + N_GROUPS * EXPERTS_PER_GROUP


def _route(x, g_ref, wr_ref, br_ref, hp_ref, gid_ref):
    tm = x.shape[0]
    h = _rms(x, g_ref[...])
    h_hi = h.astype(BF16)
    h_lo = (h - h_hi.astype(F32)).astype(BF16)
    nr = ROUTER_ROWS
    l1 = _dot_nt(wr_ref[...], h_hi)
    l2 = _dot_nt(wr_ref[0:nr, :], h_lo)
    logits = l1[0:nr] + l1[nr:2 * nr] + l2 + br_ref[...]

    rows = lax.broadcasted_iota(I32, (SUBLANES, tm), 0).astype(F32)
    big = np.float32(1 << 20)
    gl = jnp.where(rows < N_GROUPS, logits[0:SUBLANES], NEG)
    gmax = jnp.max(gl, axis=0, keepdims=True)
    g_p = 1.0 / jnp.sum(jnp.exp(gl - gmax), axis=0, keepdims=True)
    g_idx = jnp.min(jnp.where(gl == gmax, rows, big), axis=0, keepdims=True)

    el = logits[SUBLANES:2 * SUBLANES]
    for gi in range(1, N_GROUPS):
        el = jnp.where(g_idx == gi, logits[SUBLANES * (gi + 1):SUBLANES * (gi + 2)], el)
    emax = jnp.max(el, axis=0, keepdims=True)
    esum = jnp.sum(jnp.exp(el - emax), axis=0, keepdims=True)
    i0 = jnp.min(jnp.where(el == emax, rows, big), axis=0, keepdims=True)
    el2 = jnp.where(rows == i0, -jnp.inf, el)
    m2 = jnp.max(el2, axis=0, keepdims=True)
    i1 = jnp.min(jnp.where(el2 == m2, rows, big), axis=0, keepdims=True)
    p0 = 1.0 / esum
    p1 = jnp.exp(m2 - emax) / esum
    w0 = g_p * p0 / (p0 + p1)
    w1 = g_p * p1 / (p0 + p1)
    gates = jnp.where(rows == i0, w0, 0.0) + jnp.where(rows == i1, w1, 0.0)

    gates_t = jnp.concatenate([gates, jnp.zeros((LANES - SUBLANES, tm), F32)], axis=0).T
    packed = _pack_pairs(h)
    for j in range(PACK_ROWS):
        hp_ref[_slab_rows(j, tm), :] = packed[:, j * LANES:(j + 1) * LANES]
    hp_ref[_slab_rows(PACK_ROWS, tm), :] = pltpu.bitcast(gates_t, U32)
    for j in range(PACK_ROWS + 1, SUBLANES):
        hp_ref[_slab_rows(j, tm), :] = jnp.zeros((tm, LANES), U32)
    gid_ref[...] = g_idx.astype(I32).reshape(1, 1, tm)


def _moe_kernel(padded, spad_ref, poffs_ref, ntiles_ref, hp_ref, win_ref, wout_ref, o_ref, xt_ref, yt_ref):
    c = pl.program_id(0)
    g = pl.program_id(1)
    seg0 = c * padded + poffs_ref[c * N_GROUPS + g]
    tr = MOE_TILE
    st = MOE_STRIDE

    def tile_body(t, carry):
        r0 = seg0 + t * tr

        def slab(entry):
            return pl.ds(pl.multiple_of(spad_ref[entry], SUBLANES), SUBLANES)

        def gather(i, cr):
            rr, entry = i * MOE_UNROLL, r0 + i * MOE_UNROLL
            for uu in range(MOE_UNROLL):
                xt_ref[pl.ds(rr + uu, SUBLANES, stride=st), :] = hp_ref[slab(entry + uu), :]
            return cr

        lax.fori_loop(0, tr // MOE_UNROLL, gather, 0)

        packed = jnp.concatenate([xt_ref[pl.ds(j * st, tr), :] for j in range(PACK_ROWS)], axis=1)
        xb = _unpack_pairs(packed).astype(BF16)
        gates = pltpu.bitcast(xt_ref[pl.ds(PACK_ROWS * st, tr), :], F32)
        acts = []
        for e in range(EXPERTS_PER_GROUP):
            hu = _dot(xb, win_ref[e])
            hg = hu[:, 0:D_EXPERT]
            acts.append((hg * _sigmoid(hg) * hu[:, D_EXPERT:] * gates[:, e:e + 1]).astype(BF16))
        y = _dot(jnp.concatenate(acts, axis=1), wout_ref[...].reshape(GROUP_FF, D_MODEL))
        for j in range(SUBLANES):
            yt_ref[pl.ds(j * st, tr), :] = y[:, j * LANES:(j + 1) * LANES]

        def scatter(i, cr):
            rr, entry = i * MOE_UNROLL, r0 + i * MOE_UNROLL
            for uu in range(MOE_UNROLL):
                o_ref[slab(entry + uu), :] = yt_ref[pl.ds(rr + uu, SUBLANES, stride=st), :]
            return cr

        lax.fori_loop(0, tr // MOE_UNROLL, scatter, 0)
        return carry

    lax.fori_loop(0, ntiles_ref[c * N_GROUPS + g], tile_body, 0)


def _moe(l, hp, spad, poffs, ntiles, c, chunk):
    t = hp.shape[0] // SUBLANES
    win, wout = c["win"], c["wout"]
    grid_spec = pltpu.PrefetchScalarGridSpec(
        num_scalar_prefetch=3,
        grid=(t // chunk, N_GROUPS),
        in_specs=[pl.BlockSpec((chunk * SUBLANES, LANES), lambda ci, g, *_: (ci, 0)),
                  pl.BlockSpec((None, EXPERTS_PER_GROUP) + win.shape[2:], lambda ci, g, *_: (l, g, 0, 0)),
                  pl.BlockSpec((None, EXPERTS_PER_GROUP) + wout.shape[2:], lambda ci, g, *_: (l, g, 0, 0))],
        out_specs=pl.BlockSpec((chunk * SUBLANES, LANES), lambda ci, g, *_: (ci, 0)),
        scratch_shapes=[pltpu.VMEM((MOE_STRIDE * SUBLANES, LANES), U32),
                        pltpu.VMEM((MOE_STRIDE * SUBLANES, LANES), F32)],
    )
    return pl.pallas_call(
        functools.partial(_moe_kernel, spad.shape[0] // (t // chunk)),
        grid_spec=grid_spec,
        out_shape=jax.ShapeDtypeStruct((t * SUBLANES, LANES), F32),
        compiler_params=_params(2),
        name="moe",
    )(spad, poffs, ntiles, hp, win, wout)


def _ple_kernel(x_ref, mo_ref, p_ref, g_ref, wpi_ref, wg_ref, o_ref):
    tm = x_ref.shape[0]
    mo = jnp.concatenate([mo_ref[_slab_rows(j, tm), :] for j in range(SUBLANES)], axis=1)
    x = x_ref[...] + mo
    hn = _rms(x, g_ref[...]).astype(BF16)
    gate = _sigmoid(_dot(hn, wg_ref[...]))
    emb = _dot(p_ref[...].astype(BF16), wpi_ref[...])
    o_ref[...] = x + emb * gate


def _ple(l, x2d, mo, p3d, c):
    t = x2d.shape[0]
    tm = ROWS_PROJ
    row = lambda w: pl.BlockSpec((tm, w), lambda i: (i, 0))
    layer = [c["g"], c["wpi"], c["wg"]]
    return pl.pallas_call(
        _ple_kernel,
        grid=(t // tm,),
        in_specs=[row(D_MODEL), pl.BlockSpec((tm * SUBLANES, LANES), lambda i: (i, 0)),
                  pl.BlockSpec((None, tm, PLE_DIM), lambda i: (l, i, 0))] + [_layer_spec(w, l) for w in layer],
        out_specs=row(D_MODEL),
        out_shape=jax.ShapeDtypeStruct((t, D_MODEL), F32),
        compiler_params=_params(1),
        name="ple",
    )(x2d, mo, p3d, *layer)


def _bias_per_offset(rel_bias):
    r = ROWS_ATT
    nk = (HALO_BLOCKS + 1) * r
    p = nk + r
    off = np.arange(p)
    off = np.where(off < nk, off, off - p)
    rel = np.clip(HALO_BLOCKS * r - off, -REL_CLIP, REL_CLIP) + REL_CLIP
    return rel_bias[..., rel].astype(F32) * np.float32(LOG2E)


def _pool_inverse_counts():
    r = ROWS_MIX
    win = np.repeat(np.asarray(POOL_WINDOWS, np.float32), POOL_GROUP_DIM)[None, :]
    pos = np.arange(1, r + 1, dtype=np.float32)[:, None]
    return jnp.asarray(1.0 / np.minimum(pos, win), F32), jnp.asarray(1.0 / win, F32)


def _block_diag(w):
    nl, g, c, d = w.shape
    eye = jnp.eye(g, dtype=w.dtype)
    return jnp.einsum("gh,lgcd->lgchd", eye, w).reshape(nl, g * c, g * d)


def _prepare(prm):
    nl = prm["w_in"].shape[0]
    row = lambda v: v.reshape(nl, 1, -1).astype(F32)
    hm = _block_diag(jnp.full((1, MXU_DIM // HEAD_DIM, HEAD_DIM, HEAD_DIM), 1.0 / HEAD_DIM, F32))[0].astype(BF16)
    inproj = dict(
        g=row(prm["mix_norm_g"]), w_in=prm["w_in"].astype(BF16), sgn=row(prm["sgu_norm_g"]),
        gq=row(jnp.tile(prm["q_norm_g"], (1, ATT_HEADS))), gk=row(jnp.tile(prm["k_norm_g"], (1, ATT_HEADS))),
        hm=hm, gb=row(prm["gate_b"]))

    i = np.arange(SGU_BLOCK)
    tri = (i[None, :] // CHUNK) <= (i[:, None] // CHUNK)
    sw = jnp.where(jnp.asarray(tri), prm["sgu_w"], 0.0)
    swc = jnp.concatenate([sw[:, g] for g in range(SGU_GROUPS)], axis=2).astype(BF16)
    sbm = jnp.repeat(jnp.swapaxes(prm["sgu_b"], 1, 2), SGU_GROUP_DIM, axis=2).astype(F32)
    invf, invr = _pool_inverse_counts()
    mixer = dict(invf=invf, invr=invr, bd=_block_diag(prm["pool_w"]).astype(BF16), ps=row(prm["pool_scale"]),
                 swc=swc, sbm=sbm, bias=_bias_per_offset(prm["rel_bias"]),
                 wa=prm["w_branch_a"].astype(BF16), wb=prm["w_branch_b"].astype(BF16),
                 wc=prm["w_branch_c"].astype(BF16), wo=prm["w_out"].astype(BF16))

    pad = jnp.zeros((nl, SUBLANES - N_GROUPS, D_MODEL), F32)
    wr = jnp.concatenate([jnp.swapaxes(prm["w_group_router"], 1, 2), pad,
                          jnp.swapaxes(prm["w_expert_router"], 1, 2)], axis=1)
    wr_hi = wr.astype(BF16)
    wr_lo = (wr - wr_hi.astype(F32)).astype(BF16)
    br = jnp.concatenate([prm["b_group_router"], jnp.zeros((nl, SUBLANES - N_GROUPS), F32),
                          prm["b_expert_router"]], axis=1).reshape(nl, -1, 1).astype(F32)
    mixer.update(gf=row(prm["ffn_norm_g"]), wr=jnp.concatenate([wr_hi, wr_lo], axis=1), br=br)

    moe = dict(win=prm["w_expert_in"].astype(BF16),
               wout=prm["w_expert_out"].astype(BF16))

    ple = dict(g=row(prm["ple_norm_g"]), wpi=prm["w_ple_in"].astype(BF16), wg=prm["w_ple_gate"].astype(BF16))
    return inproj, mixer, moe, ple


def _sort_by_group(gid, chunk):
    tr = MOE_TILE
    gid = gid.reshape(-1, chunk)
    order = jnp.argsort(gid, axis=1, stable=True).astype(I32)
    groups = jnp.arange(N_GROUPS, dtype=I32)
    cnts = jnp.sum(gid[:, :, None] == groups, axis=1).astype(I32)
    offs = jnp.cumsum(cnts, axis=1) - cnts
    ntiles = (cnts + (tr - 1)) // tr
    poffs = (jnp.cumsum(ntiles, axis=1) - ntiles) * tr
    slot = jnp.arange(chunk + N_GROUPS * tr, dtype=I32)[None, :, None]
    seg = jnp.sum(slot >= poffs[:, None, 1:], axis=2)
    pick = lambda v: jnp.sum(jnp.where(seg[:, :, None] == groups, v[:, None, :], 0), axis=2)
    src = pick(offs) + jnp.clip(slot[:, :, 0] - pick(poffs), 0, jnp.maximum(pick(cnts) - 1, 0))
    spad = jnp.take_along_axis(order, jnp.minimum(src, chunk - 1), axis=1) * SUBLANES
    return spad.reshape(-1), poffs.reshape(-1).astype(I32), ntiles.reshape(-1).astype(I32)


def kernel(x, p, mix_norm_g, w_in, pool_w, pool_scale, sgu_norm_g, sgu_w, sgu_b, q_norm_g, k_norm_g, rel_bias, gate_b, w_branch_a, w_branch_b, w_branch_c, w_out, ffn_norm_g, w_group_router, b_group_router, w_expert_router, b_expert_router, w_expert_in, w_expert_out, ple_norm_g, w_ple_in, w_ple_gate):
    prm = dict(mix_norm_g=mix_norm_g, w_in=w_in, pool_w=pool_w, pool_scale=pool_scale, sgu_norm_g=sgu_norm_g,
               sgu_w=sgu_w, sgu_b=sgu_b, q_norm_g=q_norm_g, k_norm_g=k_norm_g, rel_bias=rel_bias, gate_b=gate_b,
               w_branch_a=w_branch_a, w_branch_b=w_branch_b, w_branch_c=w_branch_c, w_out=w_out,
               ffn_norm_g=ffn_norm_g, w_group_router=w_group_router, b_group_router=b_group_router,
               w_expert_router=w_expert_router, b_expert_router=b_expert_router, w_expert_in=w_expert_in,
               w_expert_out=w_expert_out, ple_norm_g=ple_norm_g, w_ple_in=w_ple_in, w_ple_gate=w_ple_gate)
    b, s, d = x.shape
    t = b * s
    depth = p.shape[0]
    chunk = min(MOE_CHUNK, t)
    assert d == D_MODEL and s % ROWS_MIX == 0 and t % ROWS_PROJ == 0 and t % chunk == 0
    assert ROWS_MIX % ROWS_ATT == 0 and ROWS_MIX // ROWS_ATT >= HALO_BLOCKS
    c_in, c_mix, c_moe, c_ple = _prepare(prm)
    p3d = p.reshape(depth, t, PLE_DIM)
    xf = x.reshape(t, d)
    for l in range(depth):
        xf, hp, gid = _front(l, xf, c_in, c_mix, s)
        spad, poffs, ntiles = _sort_by_group(gid, chunk)
        mo = _moe(l, hp, spad, poffs, ntiles, c_moe, chunk)
        xf = _ple(l, xf, mo, p3d, c_ple)
    return xf.reshape(b, s, d)
```

```python
import functools

import jax
import jax.numpy as jnp
import numpy as np
from jax import lax
from jax.experimental import pallas as pl
from jax.experimental.pallas import tpu as pltpu

F32 = jnp.float32
BF16 = jnp.bfloat16
U32 = jnp.uint32
I32 = jnp.int32

D_MODEL = 1024
CHUNK = 64
PLE_DIM = 256
EPS = 1e-6
POOL_WINDOWS = (2, 4, 8, 16)
POOL_WIDTH = 256
POOL_GROUP_DIM = 64
SGU_BLOCK = 128
SGU_WIDTH = 256
SGU_GROUPS = 4
SGU_GROUP_DIM = 64
HEAD_DIM = 64
ATT_WIDTH = 512
ATT_HEADS = 8
LEFT_CHUNKS = 8
REL_CLIP = 128
N_BRANCH = 3
N_GROUPS = 4
EXPERTS_PER_GROUP = 8
D_EXPERT = 128
GROUP_FF = EXPERTS_PER_GROUP * D_EXPERT

C_A, C_U, C_V, C_Q, C_K, C_VV, C_G = 0, 256, 512, 768, 1280, 1792, 2304

LANES = 128
SUBLANES = 8
MXU_DIM = 256
VMEM_LIMIT = 60 * 1024 * 1024

ROWS_PROJ = 1024
ROWS_MIX = 512
ROWS_ATT = 256
HALO_BLOCKS = 2
POOL_HALO = 16
MOE_CHUNK = 2048
MOE_TILE = 256
MOE_STRIDE = MOE_TILE + 1
MOE_UNROLL = 32
PACK_W = D_MODEL // 2
PACK_ROWS = PACK_W // LANES
NEG = -1e30
GATE_SEG = 512
N_GATE_SEGS = N_BRANCH * D_MODEL // GATE_SEG
GATE_UNITS = 6
LOG2E = float(np.log2(np.e))


def _const_spec(shape):
    nd = len(shape)
    return pl.BlockSpec(shape, lambda *_: (0,) * nd, pipeline_mode=pl.Buffered(1))


def _layer_spec(arr, l):
    nd = arr.ndim - 1
    return pl.BlockSpec((None,) + arr.shape[1:], lambda *_: (l,) + (0,) * nd, pipeline_mode=pl.Buffered(1))


def _params(n_axes):
    return pltpu.CompilerParams(dimension_semantics=("arbitrary",) * n_axes,
                                vmem_limit_bytes=VMEM_LIMIT)


def _rms(x, g):
    ms = jnp.mean(x * x, axis=-1, keepdims=True)
    return x * lax.rsqrt(ms + EPS) * g


def _gelu_tanh(x):
    c = np.float32(np.sqrt(2.0 / np.pi))
    return 0.5 * x * (1.0 + jnp.tanh(c * (x + np.float32(0.044715) * (x * x * x))))


def _sigmoid(x):
    return 0.5 * jnp.tanh(0.5 * x) + 0.5


def _dot(a, b):
    return jnp.dot(a, b, preferred_element_type=F32)


def _dot_nt(a, b):
    return lax.dot_general(a, b, (((1,), (1,)), ((), ())), preferred_element_type=F32)


def _pack_pairs(x):
    w = x.shape[1] // 2
    bits = pltpu.bitcast(x.astype(BF16).astype(F32), U32)
    return (bits[:, :w] >> 16) | (bits[:, w:] & np.uint32(0xFFFF0000))


def _unpack_pairs(p):
    lo = pltpu.bitcast(p << 16, F32)
    hi = pltpu.bitcast(p & np.uint32(0xFFFF0000), F32)
    return jnp.concatenate([lo, hi], axis=1)


def _slab_rows(j, n):
    return pl.ds(j, n, stride=SUBLANES)


def _inproj_parts(x_ref, g_ref, w_ref, sgn_ref, gq_ref, gk_ref, hm_ref, gb_ref):
    h = _rms(x_ref[...], g_ref[...]).astype(BF16)

    def seg(c0, c1):
        return _dot(h, w_ref[:, c0:c1])

    def head_norm(z, g):
        zz = (z * z).astype(BF16)
        w = hm_ref.shape[0]
        ms = jnp.concatenate([_dot(zz[:, c0:c0 + w], hm_ref[...]) for c0 in range(0, ATT_WIDTH, w)], axis=1)
        return z * lax.rsqrt(ms + EPS) * g

    def side_branches(a_ref, u_ref, v_ref):
        a_ref[...] = seg(C_A, C_U).astype(BF16)
        u_ref[...] = _gelu_tanh(seg(C_U, C_V)).astype(BF16)
        v_ref[...] = _rms(_gelu_tanh(seg(C_V, C_Q)), sgn_ref[...]).astype(BF16)

    def attention_branches(q_ref, k_ref, vv_ref):
        q_ref[...] = (head_norm(seg(C_Q, C_K), gq_ref[...]) * np.float32(HEAD_DIM ** -0.5 * LOG2E)).astype(BF16)
        k_ref[...] = head_norm(seg(C_K, C_VV), gk_ref[...]).astype(BF16)
        vv_ref[...] = seg(C_VV, C_G).astype(BF16)

    def gate_segment(i, gates_ref):
        c0 = i * GATE_SEG
        z = seg(C_G + c0, C_G + c0 + GATE_SEG) + gb_ref[:, c0:c0 + GATE_SEG]
        gates_ref[:, c0:c0 + GATE_SEG] = _sigmoid(z).astype(BF16)

    return side_branches, attention_branches, gate_segment


def _fill_attention_bias(boff_ref, bias_ref):
    ra, nk = bias_ref.shape[1], bias_ref.shape[2]
    qc = lax.broadcasted_iota(I32, (ra, nk), 0) // CHUNK
    kc = lax.broadcasted_iota(I32, (ra, nk), 1) // CHUNK
    valid = jnp.logical_and(kc >= qc, kc <= qc + LEFT_CHUNKS)
    for hd in range(bias_ref.shape[0]):
        row = jnp.broadcast_to(boff_ref[hd:hd + 1, :], (ra, boff_ref.shape[1]))
        band = pltpu.roll(row, 0, 1, stride=1, stride_axis=0)
        bias_ref[hd] = jnp.where(valid, band[:, 0:nk], NEG)


def _mixer_kernel(blocks_per_seq,
                  x_ref, a_ref, ap_ref, u_ref, v_ref, q_ref,
                  kp_ref, kc_ref, vp_ref, vc_ref, gates_ref,
                  invf_ref, invr_ref, bd_ref, ps_ref, swc_ref, sb_ref, boff_ref,
                  wa_ref, wb_ref, wc_ref, wo_ref, gf_ref, wr_ref, br_ref,
                  o_ref, hp_ref, gid_ref, pool_buf, x1_buf, bias_ref, between=None):
    r = ROWS_MIX
    step = pl.program_id(0)
    bi = step % blocks_per_seq
    first = bi == 0

    @pl.when(step == 0)
    def _():
        x1_buf[...] = jnp.zeros_like(x1_buf)
        _fill_attention_bias(boff_ref, bias_ref)

    ra = ROWS_ATT
    nsub = r // ra
    nblk = HALO_BLOCKS + 1
    lane_c = lax.broadcasted_iota(I32, (ra, LANES), 1)
    zero_q = jnp.zeros((ra, LANES), BF16)
    yc = []
    for sub in range(nsub):
        lo = (nsub + sub - HALO_BLOCKS) * ra
        sub_idx = bi * nsub + sub
        blk_ok = [sub_idx >= HALO_BLOCKS - j for j in range(HALO_BLOCKS)] + [None]
        ycs = []
        for pr in range(ATT_HEADS // 2):
            cols = slice(pr * LANES, (pr + 1) * LANES)
            qp = q_ref[sub * ra:(sub + 1) * ra, cols]
            qs = jnp.concatenate([jnp.where(lane_c < HEAD_DIM, qp, zero_q),
                                  jnp.where(lane_c >= HEAD_DIM, qp, zero_q)], axis=0)
            kcat = jnp.concatenate([kp_ref[lo:, cols], kc_ref[0:(sub + 1) * ra, cols]], axis=0)
            vcat = jnp.concatenate([vp_ref[lo:, cols], vc_ref[0:(sub + 1) * ra, cols]], axis=0)
            bias = []
            for j in range(nblk):
                bj = bias_ref[2 * pr:2 * pr + 2, :, j * ra:(j + 1) * ra].reshape(2 * ra, ra)
                bias.append(bj if blk_ok[j] is None else jnp.where(blk_ok[j], bj, NEG))
            sc = _dot_nt(qs, kcat) + jnp.concatenate(bias, axis=1)
            if between is not None:
                between(sub * (ATT_HEADS // 2) + pr)
            m = jnp.max(sc, axis=-1, keepdims=True)
            p = jnp.exp2(sc - m)
            denom = jnp.sum(p, axis=-1, keepdims=True)
            o = _dot(p.astype(BF16), vcat) / denom
            ycs.append(jnp.where(lane_c < HEAD_DIM, o[0:ra], o[ra:2 * ra]))
        yc.append(jnp.concatenate(ycs, axis=1))
    yc = jnp.concatenate(yc, axis=0).astype(BF16)

    a = a_ref[...].astype(F32)
    halo = jnp.where(first, 0.0, ap_ref[r - POOL_HALO:, :].astype(F32))
    pool_buf[0:POOL_HALO, :] = jnp.zeros((POOL_HALO, POOL_WIDTH), F32)
    pool_buf[POOL_HALO:2 * POOL_HALO, :] = halo
    pool_buf[2 * POOL_HALO:, :] = a
    lane_a = lax.broadcasted_iota(I32, (r, POOL_WIDTH), 1) // POOL_GROUP_DIM
    wsum = None
    for gi, k in enumerate((1, 2, 4, 8)):
        cur = pool_buf[POOL_HALO:, :] + pool_buf[POOL_HALO - k:r + 2 * POOL_HALO - k, :]
        pool_buf[POOL_HALO:, :] = cur
        level = cur[POOL_HALO:, :]
        wsum = level if wsum is None else jnp.where(lane_a >= gi, level, wsum)
    inv = jnp.where(first, invf_ref[...], invr_ref[...])
    pooled = (wsum * inv - a).astype(BF16)
    ya = (_dot(pooled, bd_ref[...]) * ps_ref[...]).astype(BF16)

    vb = v_ref[...]
    lane_b = lax.broadcasted_iota(I32, (SGU_BLOCK, SGU_WIDTH), 1) // SGU_GROUP_DIM
    zero_b = jnp.zeros((SGU_BLOCK, SGU_WIDTH), BF16)
    mixed = []
    for s in range(r // SGU_BLOCK):
        blk = vb[s * SGU_BLOCK:(s + 1) * SGU_BLOCK, :]
        rhs = jnp.concatenate([jnp.where(lane_b == g, blk, zero_b) for g in range(SGU_GROUPS)], axis=0)
        mixed.append(_dot(swc_ref[...], rhs) + sb_ref[...])
    yb = (u_ref[...].astype(F32) * jnp.concatenate(mixed, axis=0)).astype(BF16)

    _route(x1_buf[...], gf_ref, wr_ref, br_ref, hp_ref, gid_ref)

    d = D_MODEL
    merged = (gates_ref[:, 0:d].astype(F32) * _dot(ya, wa_ref[...])
              + gates_ref[:, d:2 * d].astype(F32) * _dot(yb, wb_ref[...])
              + gates_ref[:, 2 * d:3 * d].astype(F32) * _dot(yc, wc_ref[...]))
    x1 = x_ref[...] + _dot(merged.astype(BF16), wo_ref[...])
    o_ref[...] = x1
    x1_buf[...] = x1


N_INPROJ_REFS = 7


def _front_kernel(blocks_per_seq, x_ref, *refs):
    proj_refs, refs = refs[:N_INPROJ_REFS], refs[N_INPROJ_REFS:]
    (a_buf, u_buf, v_buf, q_buf, k_buf, vv_buf, gates_buf, pool_buf, x1_buf, bias_buf) = refs[-10:]
    mix_refs, out_refs = refs[:-13], refs[-13:-10]
    r = ROWS_MIX
    step = pl.program_id(0)
    prev, cur = pl.ds(0, r), pl.ds(r, r)
    carried = (a_buf, k_buf, vv_buf)

    @pl.when(step == 0)
    def _():
        for buf in carried:
            buf[...] = jnp.zeros_like(buf)

    live = step < pl.num_programs(0) - 1

    @pl.when(live)
    def _():
        for buf in carried:
            buf[prev, :] = buf[cur, :]
        side_branches, attention_branches, gate_segment = _inproj_parts(x_ref, *proj_refs)
        attention_branches(q_buf, k_buf.at[cur], vv_buf.at[cur])

        def between(unit):
            if unit < GATE_UNITS:
                for i in range(unit * N_GATE_SEGS // GATE_UNITS, (unit + 1) * N_GATE_SEGS // GATE_UNITS):
                    gate_segment(i, gates_buf)
            elif unit == GATE_UNITS:
                side_branches(a_buf.at[cur], u_buf, v_buf)

        _mixer_kernel(blocks_per_seq, x_ref, a_buf.at[cur], a_buf.at[prev], u_buf, v_buf, q_buf,
                      k_buf.at[prev], k_buf.at[cur], vv_buf.at[prev], vv_buf.at[cur], gates_buf,
                      *mix_refs, *out_refs, pool_buf, x1_buf, bias_buf, between=between)

    @pl.when(jnp.logical_not(live))
    def _():
        _route(x1_buf[...], *mix_refs[-3:], *out_refs[1:])


def _front(l, x2d, ci, c, seq):
    t = x2d.shape[0]
    r = ROWS_MIX
    nb = t // r
    row = lambda w: pl.BlockSpec((r, w), lambda i: (jnp.minimum(i, nb - 1), 0))
    routed = lambda shape: pl.BlockSpec(shape, lambda i: (jnp.maximum(i - 1, 0),) + (0,) * (len(shape) - 1))
    proj = [ci["g"], ci["w_in"], ci["sgn"], ci["gq"], ci["gk"]]
    layer = [c["bd"], c["ps"], c["swc"], c["sbm"], c["bias"], c["wa"], c["wb"], c["wc"], c["wo"],
             c["gf"], c["wr"], c["br"]]
    in_specs = ([row(D_MODEL)] + [_layer_spec(w, l) for w in proj]
                + [_const_spec(ci["hm"].shape), _layer_spec(ci["gb"], l),
                   _const_spec(c["invf"].shape), _const_spec(c["invr"].shape)] + [_layer_spec(w, l) for w in layer])
    vmem = lambda rows, w, dt: pltpu.VMEM((rows, w), dt)
    return pl.pallas_call(
        functools.partial(_front_kernel, seq // r),
        grid=(nb + 1,),
        in_specs=in_specs,
        out_specs=[row(D_MODEL), routed((r * SUBLANES, LANES)), routed((1, 1, r))],
        out_shape=[jax.ShapeDtypeStruct((t, D_MODEL), F32), jax.ShapeDtypeStruct((t * SUBLANES, LANES), U32),
                   jax.ShapeDtypeStruct((nb, 1, r), I32)],
        scratch_shapes=[vmem(2 * r, POOL_WIDTH, BF16), vmem(r, SGU_WIDTH, BF16), vmem(r, SGU_WIDTH, BF16),
                        vmem(r, ATT_WIDTH, BF16), vmem(2 * r, ATT_WIDTH, BF16), vmem(2 * r, ATT_WIDTH, BF16),
                        vmem(r, N_BRANCH * D_MODEL, BF16),
                        vmem(r + 2 * POOL_HALO, POOL_WIDTH, F32), vmem(r, D_MODEL, F32),
                        pltpu.VMEM((ATT_HEADS, ROWS_ATT, (HALO_BLOCKS + 1) * ROWS_ATT), F32)],
        compiler_params=_params(1),
        name="front",
    )(x2d, *proj, ci["hm"], ci["gb"], c["invf"], c["invr"], *layer)


ROUTER_ROWS = SUBLANES + N_GROUPS * EXPERTS_PER_GROUP


def _route(x, g_ref, wr_ref, br_ref, hp_ref, gid_ref):
    tm = x.shape[0]
    h = _rms(x, g_ref[...])
    h_hi = h.astype(BF16)
    h_lo = (h - h_hi.astype(F32)).astype(BF16)
    nr = ROUTER_ROWS
    l1 = _dot_nt(wr_ref[...], h_hi)
    l2 = _dot_nt(wr_ref[0:nr, :], h_lo)
    logits = l1[0:nr] + l1[nr:2 * nr] + l2 + br_ref[...]

    rows = lax.broadcasted_iota(I32, (SUBLANES, tm), 0).astype(F32)
    big = np.float32(1 << 20)
    gl = jnp.where(rows < N_GROUPS, logits[0:SUBLANES], NEG)
    gmax = jnp.max(gl, axis=0, keepdims=True)
    g_p = 1.0 / jnp.sum(jnp.exp(gl - gmax), axis=0, keepdims=True)
    g_idx = jnp.min(jnp.where(gl == gmax, rows, big), axis=0, keepdims=True)

    el = logits[SUBLANES:2 * SUBLANES]
    for gi in range(1, N_GROUPS):
        el = jnp.where(g_idx == gi, logits[SUBLANES * (gi + 1):SUBLANES * (gi + 2)], el)
    emax = jnp.max(el, axis=0, keepdims=True)
    esum = jnp.sum(jnp.exp(el - emax), axis=0, keepdims=True)
    i0 = jnp.min(jnp.where(el == emax, rows, big), axis=0, keepdims=True)
    el2 = jnp.where(rows == i0, -jnp.inf, el)
    m2 = jnp.max(el2, axis=0, keepdims=True)
    i1 = jnp.min(jnp.where(el2 == m2, rows, big), axis=0, keepdims=True)
    p0 = 1.0 / esum
    p1 = jnp.exp(m2 - emax) / esum
    w0 = g_p * p0 / (p0 + p1)
    w1 = g_p * p1 / (p0 + p1)
    gates = jnp.where(rows == i0, w0, 0.0) + jnp.where(rows == i1, w1, 0.0)

    gates_t = jnp.concatenate([gates, jnp.zeros((LANES - SUBLANES, tm), F32)], axis=0).T
    packed = _pack_pairs(h)
    for j in range(PACK_ROWS):
        hp_ref[_slab_rows(j, tm), :] = packed[:, j * LANES:(j + 1) * LANES]
    hp_ref[_slab_rows(PACK_ROWS, tm), :] = pltpu.bitcast(gates_t, U32)
    for j in range(PACK_ROWS + 1, SUBLANES):
        hp_ref[_slab_rows(j, tm), :] = jnp.zeros((tm, LANES), U32)
    gid_ref[...] = g_idx.astype(I32).reshape(1, 1, tm)


def _moe_kernel(padded, spad_ref, poffs_ref, ntiles_ref, hp_ref, win_ref, wout_ref, o_ref, xt_ref, yt_ref):
    c = pl.program_id(0)
    g = pl.program_id(1)
    seg0 = c * padded + poffs_ref[c * N_GROUPS + g]
    tr = MOE_TILE
    st = MOE_STRIDE

    def tile_body(t, carry):
        r0 = seg0 + t * tr

        def slab(entry):
            return pl.ds(pl.multiple_of(spad_ref[entry], SUBLANES), SUBLANES)

        def gather(i, cr):
            rr, entry = i * MOE_UNROLL, r0 + i * MOE_UNROLL
            for uu in range(MOE_UNROLL):
                xt_ref[pl.ds(rr + uu, SUBLANES, stride=st), :] = hp_ref[slab(entry + uu), :]
            return cr

        lax.fori_loop(0, tr // MOE_UNROLL, gather, 0)

        packed = jnp.concatenate([xt_ref[pl.ds(j * st, tr), :] for j in range(PACK_ROWS)], axis=1)
        xb = _unpack_pairs(packed).astype(BF16)
        gates = pltpu.bitcast(xt_ref[pl.ds(PACK_ROWS * st, tr), :], F32)
        acts = []
        for e in range(EXPERTS_PER_GROUP):
            hu = _dot(xb, win_ref[e])
            hg = hu[:, 0:D_EXPERT]
            acts.append((hg * _sigmoid(hg) * hu[:, D_EXPERT:] * gates[:, e:e + 1]).astype(BF16))
        y = _dot(jnp.concatenate(acts, axis=1), wout_ref[...].reshape(GROUP_FF, D_MODEL))
        for j in range(SUBLANES):
            yt_ref[pl.ds(j * st, tr), :] = y[:, j * LANES:(j + 1) * LANES]

        def scatter(i, cr):
            rr, entry = i * MOE_UNROLL, r0 + i * MOE_UNROLL
            for uu in range(MOE_UNROLL):
                o_ref[slab(entry + uu), :] = yt_ref[pl.ds(rr + uu, SUBLANES, stride=st), :]
            return cr

        lax.fori_loop(0, tr // MOE_UNROLL, scatter, 0)
        return carry

    lax.fori_loop(0, ntiles_ref[c * N_GROUPS + g], tile_body, 0)


def _moe(l, hp, spad, poffs, ntiles, c, chunk):
    t = hp.shape[0] // SUBLANES
    win, wout = c["win"], c["wout"]
    grid_spec = pltpu.PrefetchScalarGridSpec(
        num_scalar_prefetch=3,
        grid=(t // chunk, N_GROUPS),
        in_specs=[pl.BlockSpec((chunk * SUBLANES, LANES), lambda ci, g, *_: (ci, 0)),
                  pl.BlockSpec((None, EXPERTS_PER_GROUP) + win.shape[2:], lambda ci, g, *_: (l, g, 0, 0)),
                  pl.BlockSpec((None, EXPERTS_PER_GROUP) + wout.shape[2:], lambda ci, g, *_: (l, g, 0, 0))],
        out_specs=pl.BlockSpec((chunk * SUBLANES, LANES), lambda ci, g, *_: (ci, 0)),
        scratch_shapes=[pltpu.VMEM((MOE_STRIDE * SUBLANES, LANES), U32),
                        pltpu.VMEM((MOE_STRIDE * SUBLANES, LANES), F32)],
    )
    return pl.pallas_call(
        functools.partial(_moe_kernel, spad.shape[0] // (t // chunk)),
        grid_spec=grid_spec,
        out_shape=jax.ShapeDtypeStruct((t * SUBLANES, LANES), F32),
        compiler_params=_params(2),
        name="moe",
    )(spad, poffs, ntiles, hp, win, wout)


def _ple_kernel(x_ref, mo_ref, p_ref, g_ref, wpi_ref, wg_ref, o_ref):
    tm = x_ref.shape[0]
    mo = jnp.concatenate([mo_ref[_slab_rows(j, tm), :] for j in range(SUBLANES)], axis=1)
    x = x_ref[...] + mo
    hn = _rms(x, g_ref[...]).astype(BF16)
    gate = _sigmoid(_dot(hn, wg_ref[...]))
    emb = _dot(p_ref[...].astype(BF16), wpi_ref[...])
    o_ref[...] = x + emb * gate


def _ple(l, x2d, mo, p3d, c):
    t = x2d.shape[0]
    tm = ROWS_PROJ
    row = lambda w: pl.BlockSpec((tm, w), lambda i: (i, 0))
    layer = [c["g"], c["wpi"], c["wg"]]
    return pl.pallas_call(
        _ple_kernel,
        grid=(t // tm,),
        in_specs=[row(D_MODEL), pl.BlockSpec((tm * SUBLANES, LANES), lambda i: (i, 0)),
                  pl.BlockSpec((None, tm, PLE_DIM), lambda i: (l, i, 0))] + [_layer_spec(w, l) for w in layer],
        out_specs=row(D_MODEL),
        out_shape=jax.ShapeDtypeStruct((t, D_MODEL), F32),
        compiler_params=_params(1),
        name="ple",
    )(x2d, mo, p3d, *layer)


def _bias_per_offset(rel_bias):
    r = ROWS_ATT
    nk = (HALO_BLOCKS + 1) * r
    p = nk + r
    off = np.arange(p)
    off = np.where(off < nk, off, off - p)
    rel = np.clip(HALO_BLOCKS * r - off, -REL_CLIP, REL_CLIP) + REL_CLIP
    return rel_bias[..., rel].astype(F32) * np.float32(LOG2E)


def _pool_inverse_counts():
    r = ROWS_MIX
    win = np.repeat(np.asarray(POOL_WINDOWS, np.float32), POOL_GROUP_DIM)[None, :]
    pos = np.arange(1, r + 1, dtype=np.float32)[:, None]
    return jnp.asarray(1.0 / np.minimum(pos, win), F32), jnp.asarray(1.0 / win, F32)


def _block_diag(w):
    nl, g, c, d = w.shape
    eye = jnp.eye(g, dtype=w.dtype)
    return jnp.einsum("gh,lgcd->lgchd", eye, w).reshape(nl, g * c, g * d)


def _prepare(prm):
    nl = prm["w_in"].shape[0]
    row = lambda v: v.reshape(nl, 1, -1).astype(F32)
    hm = _block_diag(jnp.full((1, MXU_DIM // HEAD_DIM, HEAD_DIM, HEAD_DIM), 1.0 / HEAD_DIM, F32))[0].astype(BF16)
    inproj = dict(
        g=row(prm["mix_norm_g"]), w_in=prm["w_in"].astype(BF16), sgn=row(prm["sgu_norm_g"]),
        gq=row(jnp.tile(prm["q_norm_g"], (1, ATT_HEADS))), gk=row(jnp.tile(prm["k_norm_g"], (1, ATT_HEADS))),
        hm=hm, gb=row(prm["gate_b"]))

    i = np.arange(SGU_BLOCK)
    tri = (i[None, :] // CHUNK) <= (i[:, None] // CHUNK)
    sw = jnp.where(jnp.asarray(tri), prm["sgu_w"], 0.0)
    swc = jnp.concatenate([sw[:, g] for g in range(SGU_GROUPS)], axis=2).astype(BF16)
    sbm = jnp.repeat(jnp.swapaxes(prm["sgu_b"], 1, 2), SGU_GROUP_DIM, axis=2).astype(F32)
    invf, invr = _pool_inverse_counts()
    mixer = dict(invf=invf, invr=invr, bd=_block_diag(prm["pool_w"]).astype(BF16), ps=row(prm["pool_scale"]),
                 swc=swc, sbm=sbm, bias=_bias_per_offset(prm["rel_bias"]),
                 wa=prm["w_branch_a"].astype(BF16), wb=prm["w_branch_b"].astype(BF16),
                 wc=prm["w_branch_c"].astype(BF16), wo=prm["w_out"].astype(BF16))

    pad = jnp.zeros((nl, SUBLANES - N_GROUPS, D_MODEL), F32)
    wr = jnp.concatenate([jnp.swapaxes(prm["w_group_router"], 1, 2), pad,
                          jnp.swapaxes(prm["w_expert_router"], 1, 2)], axis=1)
    wr_hi = wr.astype(BF16)
    wr_lo = (wr - wr_hi.astype(F32)).astype(BF16)
    br = jnp.concatenate([prm["b_group_router"], jnp.zeros((nl, SUBLANES - N_GROUPS), F32),
                          prm["b_expert_router"]], axis=1).reshape(nl, -1, 1).astype(F32)
    mixer.update(gf=row(prm["ffn_norm_g"]), wr=jnp.concatenate([wr_hi, wr_lo], axis=1), br=br)

    moe = dict(win=prm["w_expert_in"].astype(BF16),
               wout=prm["w_expert_out"].astype(BF16))

    ple = dict(g=row(prm["ple_norm_g"]), wpi=prm["w_ple_in"].astype(BF16), wg=prm["w_ple_gate"].astype(BF16))
    return inproj, mixer, moe, ple


def _sort_by_group(gid, chunk):
    tr = MOE_TILE
    gid = gid.reshape(-1, chunk)
    order = jnp.argsort(gid, axis=1, stable=True).astype(I32)
    groups = jnp.arange(N_GROUPS, dtype=I32)
    cnts = jnp.sum(gid[:, :, None] == groups, axis=1).astype(I32)
    offs = jnp.cumsum(cnts, axis=1) - cnts
    ntiles = (cnts + (tr - 1)) // tr
    poffs = (jnp.cumsum(ntiles, axis=1) - ntiles) * tr
    slot = jnp.arange(chunk + N_GROUPS * tr, dtype=I32)[None, :, None]
    seg = jnp.sum(slot >= poffs[:, None, 1:], axis=2)
    pick = lambda v: jnp.sum(jnp.where(seg[:, :, None] == groups, v[:, None, :], 0), axis=2)
    src = pick(offs) + jnp.clip(slot[:, :, 0] - pick(poffs), 0, jnp.maximum(pick(cnts) - 1, 0))
    spad = jnp.take_along_axis(order, jnp.minimum(src, chunk - 1), axis=1) * SUBLANES
    return spad.reshape(-1), poffs.reshape(-1).astype(I32), ntiles.reshape(-1).astype(I32)


def kernel(x, p, mix_norm_g, w_in, pool_w, pool_scale, sgu_norm_g, sgu_w, sgu_b, q_norm_g, k_norm_g, rel_bias, gate_b, w_branch_a, w_branch_b, w_branch_c, w_out, ffn_norm_g, w_group_router, b_group_router, w_expert_router, b_expert_router, w_expert_in, w_expert_out, ple_norm_g, w_ple_in, w_ple_gate):
    prm = dict(mix_norm_g=mix_norm_g, w_in=w_in, pool_w=pool_w, pool_scale=pool_scale, sgu_norm_g=sgu_norm_g,
               sgu_w=sgu_w, sgu_b=sgu_b, q_norm_g=q_norm_g, k_norm_g=k_norm_g, rel_bias=rel_bias, gate_b=gate_b,
               w_branch_a=w_branch_a, w_branch_b=w_branch_b, w_branch_c=w_branch_c, w_out=w_out,
               ffn_norm_g=ffn_norm_g, w_group_router=w_group_router, b_group_router=b_group_router,
               w_expert_router=w_expert_router, b_expert_router=b_expert_router, w_expert_in=w_expert_in,
               w_expert_out=w_expert_out, ple_norm_g=ple_norm_g, w_ple_in=w_ple_in, w_ple_gate=w_ple_gate)
    b, s, d = x.shape
    t = b * s
    depth = p.shape[0]
    chunk = min(MOE_CHUNK, t)
    assert d == D_MODEL and s % ROWS_MIX == 0 and t % ROWS_PROJ == 0 and t % chunk == 0
    assert ROWS_MIX % ROWS_ATT == 0 and ROWS_MIX // ROWS_ATT >= HALO_BLOCKS
    c_in, c_mix, c_moe, c_ple = _prepare(prm)
    p3d = p.reshape(depth, t, PLE_DIM)
    xf = x.reshape(t, d)
    for l in range(depth):
        xf, hp, gid = _front(l, xf, c_in, c_mix, s)
        spad, poffs, ntiles = _sort_by_group(gid, chunk)
        mo = _moe(l, hp, spad, poffs, ntiles, c_moe, chunk)
        xf = _ple(l, xf, mo, p3d, c_ple)
    return xf.reshape(b, s, d)
```
